```python
import math
import jax, jax.numpy as jnp
from jax import lax
import numpy as np

D_MODEL = 1024
BATCH = 4
SEQ = 8192
DEPTH = 1

CHUNK = 64
EPS = 1e-6
PLE_DIM = 256
D_S5 = D_MODEL
S5_GROUP_CH = 16
S5_GROUPS = D_S5 // S5_GROUP_CH
S5_STATE = 64
D_SSD = D_MODEL
SSD_HEAD_DIM = 64
SSD_HEADS = D_SSD // SSD_HEAD_DIM
SSD_GROUPS = 4
SSD_HEADS_PER_GROUP = SSD_HEADS // SSD_GROUPS
SSD_STATE = 128
CONV_WIDTH = 4
SSD_CONV_DIM = D_SSD + 2 * SSD_GROUPS * SSD_STATE
NORM_GROUP = D_SSD // SSD_GROUPS
MIX_WIDTH = D_S5 + D_SSD
D_IN_PROJ = 2 * D_S5 + D_SSD + SSD_CONV_DIM + SSD_HEADS
SPLITS = (D_S5, 2 * D_S5, 2 * D_S5 + D_SSD, 2 * D_S5 + D_SSD + SSD_CONV_DIM)

kernel_name = "hybrid_s5_ssd_parallel_heads"


def rmsnorm(x, w):
    xf = x.astype(jnp.float32)
    y = xf * lax.rsqrt(jnp.mean(xf * xf, axis=-1, keepdims=True) + EPS)
    return (y * w.astype(jnp.float32)).astype(x.dtype)


def gated_group_rmsnorm(y, z, w):
    g = y.astype(jnp.float32) * jax.nn.silu(z.astype(jnp.float32))
    b, l, d = g.shape
    g = g.reshape(b, l, SSD_GROUPS, NORM_GROUP)
    g = g * lax.rsqrt(jnp.mean(g * g, axis=-1, keepdims=True) + EPS)
    return (g.reshape(b, l, d) * w.astype(jnp.float32)).astype(y.dtype)


def causal_depthwise_conv(x, w, bias):
    k_w = w.shape[0]
    L = x.shape[1]
    xp = jnp.pad(x, ((0, 0), (k_w - 1, 0), (0, 0)))
    return sum(xp[:, k:k + L] * w[k] for k in range(k_w)) + bias


def segsum(a):
    T = a.shape[-1]
    cs = jnp.cumsum(a, axis=-1)
    diff = cs[..., :, None] - cs[..., None, :]
    mask = jnp.tril(jnp.ones((T, T), dtype=bool))
    return jnp.where(mask, diff, -jnp.inf)


def s5_branch(u, A_re, A_im, log_dt, B_re, B_im, C_re, C_im, D, w_glu, b_glu):
    f32 = jnp.float32
    b, L, _ = u.shape
    n_chunks = L // CHUNK
    uf = u.astype(f32)
    u_c = uf.reshape(b, n_chunks, CHUNK, S5_GROUPS, S5_GROUP_CH).transpose(1, 0, 2, 3, 4)
    A = lax.complex(A_re.astype(f32), A_im.astype(f32))
    dt = jnp.exp(log_dt.astype(f32))[:, None]
    A_bar = jnp.exp(A * dt)
    Bc = lax.complex(B_re.astype(f32), B_im.astype(f32))
    B_bar = ((A_bar - 1.0) / A)[..., None] * Bc
    Cc = lax.complex(C_re.astype(f32), C_im.astype(f32))
    steps = jnp.arange(1, CHUNK + 1, dtype=f32)[:, None, None]
    pows = jnp.exp(A[None] * dt[None] * steps)

    def combine(e1, e2):
        a1, b1 = e1
        a2, b2 = e2
        return a1 * a2, a2 * b1 + b2

    def step(h0, uc):
        bu = jnp.einsum('btgh,gnh->btgn', uc, B_bar)
        a = jnp.broadcast_to(A_bar, bu.shape)
        _, local = lax.associative_scan(combine, (a, bu), axis=1)
        states = local + pows * h0[:, None]
        y = jnp.einsum('btgn,ghn->btgh', states, Cc).real
        return states[:, -1], y

    h0 = jnp.zeros((b, S5_GROUPS, S5_STATE), jnp.complex64)
    _, ys = lax.scan(step, h0, u_c)
    y = ys.transpose(1, 0, 2, 3, 4).reshape(b, L, D_S5) + D.astype(f32) * uf
    y = jax.nn.gelu(y)
    y = y * jax.nn.sigmoid(y @ w_glu.astype(f32) + b_glu.astype(f32))
    return y.astype(u.dtype)


def ssd_chunked(xs, dt, A, Bm, Cm):
    b, L, H, P = xs.shape
    c = L // CHUNK
    Bh = jnp.repeat(Bm, SSD_HEADS_PER_GROUP, axis=2).reshape(b, c, CHUNK, H, SSD_STATE)
    Ch = jnp.repeat(Cm, SSD_HEADS_PER_GROUP, axis=2).reshape(b, c, CHUNK, H, SSD_STATE)
    xd = (xs * dt[..., None]).reshape(b, c, CHUNK, H, P)
    a = (dt * A).reshape(b, c, CHUNK, H).transpose(0, 3, 1, 2)
    a_cs = jnp.cumsum(a, axis=-1)
    L_intra = jnp.exp(segsum(a))
    scores = jnp.einsum('bclhn,bcshn->bhcls', Ch, Bh) * L_intra
    y_diag = jnp.einsum('bhcls,bcshp->bclhp', scores, xd)
    decay_states = jnp.exp(a_cs[..., -1:] - a_cs)
    states = jnp.einsum('bclhn,bhcl,bclhp->bchpn', Bh, decay_states, xd)
    states = jnp.concatenate([jnp.zeros_like(states[:, :1]), states], axis=1)
    chunk_a = jnp.pad(a_cs[..., -1], ((0, 0), (0, 0), (1, 0)))
    decay_chunk = jnp.exp(segsum(chunk_a))
    new_states = jnp.einsum('bhzc,bchpn->bzhpn', decay_chunk, states)
    prev_states = new_states[:, :-1]
    y_off = jnp.einsum('bclhn,bchpn,bhcl->bclhp', Ch, prev_states, jnp.exp(a_cs))
    return (y_diag + y_off).reshape(b, L, H, P)


def ssd_branch(xBC, dt_raw, z, conv_w, conv_b, dt_bias, A_log, Dh, norm_w):
    f32 = jnp.float32
    b, L, _ = xBC.shape
    xBC = jax.nn.silu(causal_depthwise_conv(xBC, conv_w, conv_b))
    xs = xBC[..., :D_SSD].astype(f32).reshape(b, L, SSD_HEADS, SSD_HEAD_DIM)
    Bm = xBC[..., D_SSD:D_SSD + SSD_GROUPS * SSD_STATE].astype(f32).reshape(b, L, SSD_GROUPS, SSD_STATE)
    Cm = xBC[..., D_SSD + SSD_GROUPS * SSD_STATE:].astype(f32).reshape(b, L, SSD_GROUPS, SSD_STATE)
    dt = jax.nn.softplus(dt_raw.astype(f32) + dt_bias.astype(f32))
    A = -jnp.exp(A_log.astype(f32))
    y = ssd_chunked(xs, dt, A, Bm, Cm) + Dh.astype(f32)[:, None] * xs
    y = y.reshape(b, L, D_SSD).astype(z.dtype)
    return gated_group_rmsnorm(y, z, norm_w)


def setup_inputs(seed: int = 0) -> dict:
    key = jax.random.key(seed)
    ks = jax.random.split(key, 26)
    f32 = jnp.float32
    nrm = lambda k, s, sc: jax.random.normal(k, s, f32) * sc
    x = jax.random.normal(ks[0], (BATCH, SEQ, D_MODEL), f32)
    p = jax.random.normal(ks[1], (DEPTH, BATCH, SEQ, PLE_DIM), f32)
    norm_w = 1.0 + nrm(ks[2], (DEPTH, D_MODEL), 0.02)
    w_in = nrm(ks[3], (DEPTH, D_MODEL, D_IN_PROJ), D_MODEL ** -0.5)
    n_idx = jnp.arange(S5_STATE, dtype=f32)
    s5_A_re = -0.5 + nrm(ks[4], (DEPTH, S5_GROUPS, S5_STATE), 0.01)
    s5_A_im = math.pi * n_idx + nrm(ks[5], (DEPTH, S5_GROUPS, S5_STATE), 0.01)
    s5_log_dt = jax.random.uniform(ks[6], (DEPTH, S5_GROUPS), f32, math.log(1e-3), math.log(1e-1))
    s5_B_re = nrm(ks[7], (DEPTH, S5_GROUPS, S5_STATE, S5_GROUP_CH), (2 * S5_GROUP_CH) ** -0.5)
    s5_B_im = nrm(ks[8], (DEPTH, S5_GROUPS, S5_STATE, S5_GROUP_CH), (2 * S5_GROUP_CH) ** -0.5)
    s5_C_re = nrm(ks[9], (DEPTH, S5_GROUPS, S5_GROUP_CH, S5_STATE), (2 * S5_STATE) ** -0.5)
    s5_C_im = nrm(ks[10], (DEPTH, S5_GROUPS, S5_GROUP_CH, S5_STATE), (2 * S5_STATE) ** -0.5)
    s5_D = nrm(ks[11], (DEPTH, D_S5), 1.0)
    s5_w_glu = nrm(ks[12], (DEPTH, D_S5, D_S5), D_S5 ** -0.5)
    s5_b_glu = nrm(ks[13], (DEPTH, D_S5), 0.01)
    conv_w = nrm(ks[14], (DEPTH, CONV_WIDTH, SSD_CONV_DIM), CONV_WIDTH ** -0.5)
    conv_b = nrm(ks[15], (DEPTH, SSD_CONV_DIM), 0.01)
    dt0 = jnp.exp(jax.random.uniform(ks[16], (DEPTH, SSD_HEADS), f32, math.log(1e-3), math.log(1e-1)))
    dt_bias = dt0 + jnp.log(-jnp.expm1(-dt0))
    A_log = jnp.log(jax.random.uniform(ks[17], (DEPTH, SSD_HEADS), f32, 1.0, 16.0))
    ssd_D = 1.0 + nrm(ks[18], (DEPTH, SSD_HEADS), 0.1)
    ssd_norm_w = 1.0 + nrm(ks[19], (DEPTH, D_SSD), 0.02)
    w_out = nrm(ks[20], (DEPTH, MIX_WIDTH, D_MODEL), MIX_WIDTH ** -0.5)
    ple_norm_w = 1.0 + nrm(ks[21], (DEPTH, D_MODEL), 0.02)
    w_ple_gate = nrm(ks[22], (DEPTH, D_MODEL, D_MODEL), D_MODEL ** -0.5)
    w_ple_proj = nrm(ks[23], (DEPTH, PLE_DIM, D_MODEL), PLE_DIM ** -0.5)
    final_norm_w = 1.0 + nrm(ks[24], (D_MODEL,), 0.02)
    return {"x": x, "p": p, "norm_w": norm_w, "w_in": w_in,
            "s5_A_re": s5_A_re, "s5_A_im": s5_A_im, "s5_log_dt": s5_log_dt,
            "s5_B_re": s5_B_re, "s5_B_im": s5_B_im, "s5_C_re": s5_C_re, "s5_C_im": s5_C_im,
            "s5_D": s5_D, "s5_w_glu": s5_w_glu, "s5_b_glu": s5_b_glu,
            "conv_w": conv_w, "conv_b": conv_b, "dt_bias": dt_bias, "A_log": A_log,
            "ssd_D": ssd_D, "ssd_norm_w": ssd_norm_w, "w_out": w_out,
            "ple_norm_w": ple_norm_w, "w_ple_gate": w_ple_gate, "w_ple_proj": w_ple_proj,
            "final_norm_w": final_norm_w}


def reference(x, p, norm_w, w_in, s5_A_re, s5_A_im, s5_log_dt, s5_B_re, s5_B_im,
              s5_C_re, s5_C_im, s5_D, s5_w_glu, s5_b_glu, conv_w, conv_b, dt_bias,
              A_log, ssd_D, ssd_norm_w, w_out, ple_norm_w, w_ple_gate, w_ple_proj,
              final_norm_w):
    h = x
    for i in range(DEPTH):
        hn = rmsnorm(h, norm_w[i])
        proj = hn @ w_in[i]
        u_s5, z_s5, z_ssd, xBC, dt_raw = jnp.split(proj, SPLITS, axis=-1)
        y_s5 = s5_branch(u_s5, s5_A_re[i], s5_A_im[i], s5_log_dt[i], s5_B_re[i], s5_B_im[i],
                         s5_C_re[i], s5_C_im[i], s5_D[i], s5_w_glu[i], s5_b_glu[i])
        y_s5 = y_s5 * jax.nn.silu(z_s5)
        y_ssd = ssd_branch(xBC, dt_raw, z_ssd, conv_w[i], conv_b[i], dt_bias[i], A_log[i],
                           ssd_D[i], ssd_norm_w[i])
        h = h + jnp.concatenate([y_s5, y_ssd], axis=-1) @ w_out[i]
        gate = jax.nn.sigmoid(rmsnorm(h, ple_norm_w[i]) @ w_ple_gate[i])
        h = h + (p[i] @ w_ple_proj[i]) * gate
    return rmsnorm(h, final_norm_w)
```

```python
import functools

import jax
import jax.numpy as jnp
from jax import lax
from jax.experimental import pallas as pl
from jax.experimental.pallas import tpu as pltpu

f32 = jnp.float32
bf16 = jnp.bfloat16

EPS = 1e-6
LANES = 128
S5_GROUP_CH = 16
S5_STATE = 64
S5_CHUNK = 64
SSD_HEAD_DIM = 64
SSD_GROUPS = 4
SSD_STATE = 128
SSD_CHUNK = 128
CONV_WIDTH = 4
VMEM_LIMIT = 56 * 1024 * 1024


def _cparams(*sem):
    return pltpu.CompilerParams(dimension_semantics=sem, vmem_limit_bytes=VMEM_LIMIT)


def _const_spec(shape):
    nd = len(shape)
    return pl.BlockSpec(shape, lambda *_: (0,) * nd, pipeline_mode=pl.Buffered(1))


def _swap_halves(x):
    return pltpu.roll(x, LANES // 2, x.ndim - 1)


def _s5_params_kernel(are_ref, aim_ref, ldt_ref, bt_ref, ca_ref, cr_ref, ci_ref,
                      toep_ref, p_ref, qt_ref, dec_ref, kv_ref):
    T, H = S5_CHUNK, S5_GROUP_CH
    ar2, ai2 = are_ref[0], aim_ref[0]
    dt = jnp.exp(ldt_ref[0])
    xr, xi = ar2 * dt, ai2 * dt
    lane1 = lax.broadcasted_iota(jnp.int32, (1, LANES), 1)
    sgn = jnp.where(lane1 < T, -1.0, 1.0).astype(f32)

    def cmul(dr, di, x):
        return dr * x + di * (sgn * _swap_halves(x))

    ea = jnp.exp(xr)
    abr, abi = ea * jnp.cos(xi), ea * jnp.sin(xi)
    den = ar2 * ar2 + ai2 * ai2
    nr, ni = abr - 1.0, abi
    qr = (nr * ar2 + ni * ai2) / den
    qi = (ni * ar2 - nr * ai2) / den
    bb = cmul(qr, qi, bt_ref[0])

    tau = lax.broadcasted_iota(jnp.int32, (T, LANES), 0).astype(f32)

    def powtab(t):
        e = jnp.exp(t * xr)
        return e * jnp.cos(t * xi), e * jnp.sin(t * xi)

    pr_rev, pi_rev = powtab((T - 1.0) - tau)
    pr1, pi1 = powtab(tau + 1.0)
    pr0, pi0 = powtab(tau)
    conj = -sgn
    for h in range(H):
        p_ref[0, h * T:(h + 1) * T, :] = cmul(pr_rev, pi_rev, bb[h:h + 1, :]).astype(p_ref.dtype)
        qt_ref[0, h * T:(h + 1) * T, :] = (conj * cmul(pr1, pi1, ca_ref[0, h:h + 1, :])).astype(qt_ref.dtype)

    pr_t, pi_t = powtab(jnp.full((1, LANES), float(T), f32))
    dec_ref[0] = jnp.zeros((8, LANES), f32)
    dec_ref[0, 0:1, :] = pr_t
    dec_ref[0, 1:2, :] = sgn * pi_t

    pows = jnp.where(lax.broadcasted_iota(jnp.int32, (T, LANES), 1) < T, pr0, -pi0)
    zeros = jnp.zeros((T, LANES), f32)
    rhs = jnp.concatenate([jnp.concatenate([pows, zeros], axis=1),
                           jnp.concatenate([zeros, pows], axis=1)], axis=0)
    cb = [cr_ref[0, hp:hp + 1, :] * bb + ci_ref[0, hp:hp + 1, :] * (sgn * _swap_halves(bb)) for hp in range(H)]
    lhs = jnp.concatenate([jnp.concatenate([cb[2 * m], cb[2 * m + 1]], axis=1) for m in range(H // 2)], axis=0)
    kv_ref[...] = lax.dot_general(lhs, rhs, (((1,), (1,)), ((), ())), precision=lax.Precision.HIGHEST,
                                  preferred_element_type=f32)

    row = lax.broadcasted_iota(jnp.int32, (T, LANES), 0)
    col = lax.broadcasted_iota(jnp.int32, (T, LANES), 1)
    causal = (col % T) >= row
    for h in range(H):
        for m in range(H // 2):
            v = jnp.broadcast_to(kv_ref[m * H + h:m * H + h + 1, :], (T, LANES))
            tile = pltpu.roll(v, 0, 1, stride=1, stride_axis=0)
            toep_ref[0, h * T:(h + 1) * T, m * LANES:(m + 1) * LANES] = jnp.where(causal, tile, 0.0).astype(toep_ref.dtype)


def _s5_params(s5_A_re, s5_A_im, s5_log_dt, s5_B_re, s5_B_im, s5_C_re, s5_C_im):
    G, N = s5_A_re.shape
    H, T = S5_GROUP_CH, S5_CHUNK
    dup = lambda a: jnp.concatenate([a, a], axis=-1)
    are = dup(s5_A_re)[:, None, :]
    aim = dup(s5_A_im)[:, None, :]
    ldt = jnp.broadcast_to(s5_log_dt[:, None, None], (G, 1, 2 * N))
    bt = jnp.concatenate([jnp.swapaxes(s5_B_re, 1, 2), jnp.swapaxes(s5_B_im, 1, 2)], axis=-1)
    ca = jnp.concatenate([s5_C_re, s5_C_im], axis=-1)
    cr, ci = dup(s5_C_re), dup(s5_C_im)
    g3 = lambda a: pl.BlockSpec((1,) + a.shape[1:], lambda g: (g, 0, 0))
    ins = (are, aim, ldt, bt, ca, cr, ci)
    return pl.pallas_call(
        _s5_params_kernel,
        out_shape=(jax.ShapeDtypeStruct((G, H * T, H * T), bf16),
                   jax.ShapeDtypeStruct((G, H * T, 2 * N), bf16),
                   jax.ShapeDtypeStruct((G, H * T, 2 * N), bf16),
                   jax.ShapeDtypeStruct((G, 8, 2 * N), f32)),
        grid=(G,),
        in_specs=[g3(a) for a in ins],
        out_specs=(pl.BlockSpec((1, H * T, H * T), lambda g: (g, 0, 0)),
                   pl.BlockSpec((1, H * T, 2 * N), lambda g: (g, 0, 0)),
                   pl.BlockSpec((1, H * T, 2 * N), lambda g: (g, 0, 0)),
                   pl.BlockSpec((1, 8, 2 * N), lambda g: (g, 0, 0))),
        scratch_shapes=[pltpu.VMEM((H // 2 * H, LANES), f32)],
        compiler_params=_cparams("arbitrary"),
        name="s5_params",
    )(*ins)


def _rms(x, w):
    return x * lax.rsqrt(jnp.mean(x * x, axis=-1, keepdims=True) + EPS) * w


def _in_proj_kernel(x_ref, nw_ref, wut_ref, wr_ref, u3_ref, zs_ref, zd_ref, xbc_ref, dt_ref):
    d = x_ref.shape[1]
    hn = _rms(x_ref[...], nw_ref[...]).astype(bf16)
    r = jnp.dot(hn, wr_ref[...], preferred_element_type=f32)
    zs_ref[...] = r[:, :d].astype(zs_ref.dtype)
    zd_ref[...] = r[:, d:2 * d].astype(zd_ref.dtype)
    xbc_ref[...] = r[:, 2 * d:4 * d].astype(xbc_ref.dtype)
    dt_ref[...] = r[:, 4 * d:]
    for jj in range(x_ref.shape[0] // (2 * LANES)):
        v = lax.dot_general(wut_ref[...], hn[jj * 2 * LANES:(jj + 1) * 2 * LANES, :],
                            (((1,), (1,)), ((), ())), preferred_element_type=f32)
        u3_ref[2 * jj] = v[:, :LANES]
        u3_ref[2 * jj + 1] = v[:, LANES:]


def _in_proj(x2, norm_w, wut, wr, tb):
    m, d = x2.shape
    nr = wr.shape[1]
    return pl.pallas_call(
        _in_proj_kernel,
        out_shape=(jax.ShapeDtypeStruct((m // LANES, d, LANES), f32),
                   jax.ShapeDtypeStruct((m, d), bf16),
                   jax.ShapeDtypeStruct((m, d), bf16),
                   jax.ShapeDtypeStruct((m, 2 * d), bf16),
                   jax.ShapeDtypeStruct((m, LANES), f32)),
        grid=(m // tb,),
        in_specs=[pl.BlockSpec((tb, d), lambda i: (i, 0)),
                  _const_spec((1, d)), _const_spec((d, d)), _const_spec((d, nr))],
        out_specs=(pl.BlockSpec((tb // LANES, d, LANES), lambda i: (i, 0, 0)),
                   pl.BlockSpec((tb, d), lambda i: (i, 0)),
                   pl.BlockSpec((tb, d), lambda i: (i, 0)),
                   pl.BlockSpec((tb, 2 * d), lambda i: (i, 0)),
                   pl.BlockSpec((tb, LANES), lambda i: (i, 0))),
        compiler_params=_cparams("parallel"),
        name="in_proj",
    )(x2, norm_w, wut, wr)


def _s5_mix_kernel(u_ref, toep_ref, p_ref, qt_ref, dec_ref, d_ref, o_ref, *, rows_per_seq):
    J = u_ref.shape[0]
    H = S5_GROUP_CH
    half = LANES // 2
    lo = lax.broadcasted_iota(jnp.int32, (J, LANES), 1) < half
    e0, e1 = [], []
    for m in range(H // 2):
        a, b = u_ref[:, 2 * m, :], u_ref[:, 2 * m + 1, :]
        e0.append(jnp.where(lo, a, _swap_halves(b)))
        e1.append(jnp.where(lo, _swap_halves(a), b))
    e = jnp.concatenate([jnp.concatenate(e0, axis=1), jnp.concatenate(e1, axis=1)], axis=0)
    eb = e.astype(bf16)
    y = jnp.dot(eb, toep_ref[0], preferred_element_type=f32)
    s = jnp.dot(eb, p_ref[0], preferred_element_type=f32)
    s0, s1 = s[:J], s[J:]
    ar, ai = dec_ref[0, 0:1, :], dec_ref[0, 1:2, :]

    def cm(cr, ci, x):
        return cr * x + ci * _swap_halves(x)

    def csq(cr, ci):
        return cr * cr - ci * ci, 2.0 * cr * ci

    v = cm(ar, ai, s0) + s1
    k_idx = lax.broadcasted_iota(jnp.int32, (J, LANES), 0) % rows_per_seq
    w = v
    dr, di = csq(ar, ai)
    step = 1
    while step < rows_per_seq:
        sh = jnp.where(k_idx >= step, pltpu.roll(w, step, 0), 0.0)
        w = w + cm(dr, di, sh)
        dr, di = csq(dr, di)
        step *= 2
    h0 = jnp.where(k_idx >= 1, pltpu.roll(w, 1, 0), 0.0)
    h1 = cm(ar, ai, h0) + s0
    hb = jnp.concatenate([h0, h1], axis=0).astype(bf16)
    y = y + lax.dot_general(hb, qt_ref[0], (((1,), (1,)), ((), ())), preferred_element_type=f32)
    y = jax.nn.gelu(y + e * d_ref[0])
    for m in range(H // 2):
        y0 = y[:J, m * LANES:(m + 1) * LANES]
        y1 = y[J:, m * LANES:(m + 1) * LANES]
        o_ref[:, 2 * m, :] = jnp.where(lo, y0, _swap_halves(y1))
        o_ref[:, 2 * m + 1, :] = jnp.where(lo, _swap_halves(y0), y1)


def _s5_mix(u3, toep, p, qt, dec, drow, rows_per_seq):
    J, d, _ = u3.shape
    G = toep.shape[0]
    H = S5_GROUP_CH
    w = toep.shape[1]
    n2 = p.shape[2]
    return pl.pallas_call(
        functools.partial(_s5_mix_kernel, rows_per_seq=rows_per_seq),
        out_shape=jax.ShapeDtypeStruct((J, d, LANES), f32),
        grid=(G,),
        in_specs=[pl.BlockSpec((J, H, LANES), lambda g: (0, g, 0)),
                  pl.BlockSpec((1, w, w), lambda g: (g, 0, 0)),
                  pl.BlockSpec((1, w, n2), lambda g: (g, 0, 0)),
                  pl.BlockSpec((1, w, n2), lambda g: (g, 0, 0)),
                  pl.BlockSpec((1, 8, n2), lambda g: (g, 0, 0)),
                  pl.BlockSpec((1, 1, w), lambda g: (g, 0, 0))],
        out_specs=pl.BlockSpec((J, H, LANES), lambda g: (0, g, 0)),
        compiler_params=_cparams("parallel"),
        name="s5_mix",
    )(u3, toep, p, qt, dec, drow)


def _ssd_kernel(xbc_ref, dt_ref, zd_ref, cw_ref, cb_ref, dtb_ref, alog_ref, dexp_ref, nw_ref, e2_ref,
                o_ref, xpad_ref, state_ref):
    ts = xbc_ref.shape[0]
    d = o_ref.shape[1]
    T, N, P = SSD_CHUNK, SSD_STATE, SSD_HEAD_DIM
    gw = d // SSD_GROUPS
    hpg = gw // P
    i = pl.program_id(1)

    @pl.when(i == 0)
    def _():
        state_ref[...] = jnp.zeros_like(state_ref)
        xpad_ref[0:8, :] = jnp.zeros((8, xpad_ref.shape[1]), f32)

    @pl.when(i > 0)
    def _():
        xpad_ref[0:8, :] = xpad_ref[ts:ts + 8, :]

    xpad_ref[8:8 + ts, :] = xbc_ref[...].astype(f32)
    acc = jnp.broadcast_to(cb_ref[...], (ts, xpad_ref.shape[1]))
    for k in range(CONV_WIDTH):
        acc = acc + cw_ref[k:k + 1, :] * xpad_ref[pl.ds(8 - (CONV_WIDTH - 1) + k, ts), :]
    xc = acc * jax.nn.sigmoid(acc)

    dt = jax.nn.softplus(dt_ref[...] + dtb_ref[...])
    a = dt * (-jnp.exp(alog_ref[...]))
    e2 = e2_ref[...]

    def expand(v):
        hi = v.astype(bf16)
        lo = (v - hi.astype(f32)).astype(bf16)
        return jnp.dot(jnp.concatenate([hi, lo], axis=1), e2, preferred_element_type=f32)

    li = lax.broadcasted_iota(jnp.int32, (T, T), 0)
    si = lax.broadcasted_iota(jnp.int32, (T, T), 1)
    causal = li >= si
    tri = causal.astype(f32)
    for c in range(ts // T):
        sl = slice(c * T, (c + 1) * T)
        a_cs = jnp.dot(tri, a[sl], precision=lax.Precision.HIGHEST, preferred_element_type=f32)
        a_last = a_cs[T - 1:T, :]
        a_cst = a_cs.T
        dt_x = expand(dt[sl])
        w_x = expand(jnp.exp(a_last - a_cs))
        e_x = expand(jnp.exp(a_cs))
        cd_x = expand(jnp.broadcast_to(jnp.exp(a_last), (8, LANES)))[0:1, :]
        x_c = xc[sl, :d]
        xd = x_c * dt_x
        xdb = xd.astype(bf16)
        xwb = (xd * w_x).astype(bf16)
        ys = []
        for q in range(SSD_GROUPS):
            bq = xc[sl, d + q * N:d + (q + 1) * N].astype(bf16)
            cq = xc[sl, d + SSD_GROUPS * N + q * N:d + SSD_GROUPS * N + (q + 1) * N].astype(bf16)
            g = lax.dot_general(cq, bq, (((1,), (1,)), ((), ())), preferred_element_type=f32)
            yd = []
            for hh in range(hpg):
                h = q * hpg + hh
                diff = a_cs[:, h:h + 1] - a_cst[h:h + 1, :]
                lm = jnp.exp(jnp.where(causal, diff, -1e30))
                sc = (g * lm).astype(bf16)
                yd.append(jnp.dot(sc, xdb[:, h * P:(h + 1) * P], preferred_element_type=f32))
            cs = slice(q * gw, (q + 1) * gw)
            st = state_ref[:, cs]
            y_off = jnp.dot(cq, st.astype(bf16), preferred_element_type=f32) * e_x[:, cs]
            s_new = lax.dot_general(bq, xwb[:, cs], (((0,), (0,)), ((), ())), preferred_element_type=f32)
            state_ref[:, cs] = st * cd_x[:, cs] + s_new
            ys.append(jnp.concatenate(yd, axis=1) + y_off)
        y = jnp.concatenate(ys, axis=1) + dexp_ref[...] * x_c
        z = zd_ref[sl, :].astype(f32)
        gt = y * (z * jax.nn.sigmoid(z))
        gn = []
        for q in range(SSD_GROUPS):
            gq = gt[:, q * gw:(q + 1) * gw]
            gn.append(gq * lax.rsqrt(jnp.sum(gq * gq, axis=-1, keepdims=True) * (1.0 / gw) + EPS))
        o_ref[sl, :] = (jnp.concatenate(gn, axis=1) * nw_ref[...]).astype(o_ref.dtype)


def _ssd(xbc, dtr, zd, cw, cb, dtb, alog, dexp, nw, e2, batch, ts):
    m, cdim = xbc.shape
    d = zd.shape[1]
    nb = m // batch // ts
    tok = lambda b, i: (b * nb + i, 0)
    return pl.pallas_call(
        _ssd_kernel,
        out_shape=jax.ShapeDtypeStruct((m, d), bf16),
        grid=(batch, nb),
        in_specs=[pl.BlockSpec((ts, cdim), tok), pl.BlockSpec((ts, LANES), tok), pl.BlockSpec((ts, d), tok),
                  _const_spec(cw.shape), _const_spec(cb.shape), _const_spec(dtb.shape), _const_spec(alog.shape),
                  _const_spec(dexp.shape), _const_spec(nw.shape), _const_spec(e2.shape)],
        out_specs=pl.BlockSpec((ts, d), tok),
        scratch_shapes=[pltpu.VMEM((ts + 8, cdim), f32), pltpu.VMEM((SSD_STATE, d), f32)],
        compiler_params=_cparams("arbitrary", "arbitrary"),
        name="ssd",
    )(xbc, dtr, zd, cw, cb, dtb, alog, dexp, nw, e2)


def _out_kernel(x_ref, y3_ref, zs_ref, yd_ref, p_ref, wglu_ref, bglu_ref, wo_ref, pnw_ref, wg_ref, wp_ref, fnw_ref,
                o_ref):
    d = x_ref.shape[1]
    y = jnp.concatenate([y3_ref[jj].T for jj in range(y3_ref.shape[0])], axis=0)
    glu = y * jax.nn.sigmoid(jnp.dot(y.astype(bf16), wglu_ref[...], preferred_element_type=f32) + bglu_ref[...])
    z = zs_ref[...].astype(f32)
    ys5 = glu * (z * jax.nn.sigmoid(z))
    h = x_ref[...] + jnp.dot(ys5.astype(bf16), wo_ref[:d, :], preferred_element_type=f32)
    h = h + jnp.dot(yd_ref[...], wo_ref[d:, :], preferred_element_type=f32)
    gate = jax.nn.sigmoid(jnp.dot(_rms(h, pnw_ref[...]).astype(bf16), wg_ref[...], preferred_element_type=f32))
    h = h + jnp.dot(p_ref[...].astype(bf16), wp_ref[...], preferred_element_type=f32) * gate
    o_ref[...] = _rms(h, fnw_ref[...])


def _out_proj(x2, y3, zs, yd, p2, wglu, bglu, wo, pnw, wg, wp, fnw, tb):
    m, d = x2.shape
    tok = lambda i: (i, 0)
    return pl.pallas_call(
        _out_kernel,
        out_shape=jax.ShapeDtypeStruct((m, d), f32),
        grid=(m // tb,),
        in_specs=[pl.BlockSpec((tb, d), tok),
                  pl.BlockSpec((tb // LANES, d, LANES), lambda i: (i, 0, 0)),
                  pl.BlockSpec((tb, d), tok), pl.BlockSpec((tb, d), tok), pl.BlockSpec((tb, p2.shape[1]), tok),
                  _const_spec(wglu.shape), _const_spec(bglu.shape), _const_spec(wo.shape), _const_spec(pnw.shape),
                  _const_spec(wg.shape), _const_spec(wp.shape), _const_spec(fnw.shape)],
        out_specs=pl.BlockSpec((tb, d), tok),
        compiler_params=_cparams("parallel"),
        name="out_proj",
    )(x2, y3, zs, yd, p2, wglu, bglu, wo, pnw, wg, wp, fnw)


def _block(m, want):
    b = want
    while m % b:
        b //= 2
    return b


def kernel(x, p, norm_w, w_in, s5_A_re, s5_A_im, s5_log_dt, s5_B_re, s5_B_im, s5_C_re, s5_C_im, s5_D, s5_w_glu, s5_b_glu, conv_w, conv_b, dt_bias, A_log, ssd_D, ssd_norm_w, w_out, ple_norm_w, w_ple_gate, w_ple_proj, final_norm_w):
    batch, seq, d = x.shape
    assert norm_w.shape[0] == 1, "single-layer problem: the final norm is fused into the layer's last kernel"
    m = batch * seq
    n_heads = dt_bias.shape[1]
    assert seq % (2 * LANES) == 0 and d % LANES == 0
    rows_per_seq = seq // LANES
    assert rows_per_seq & (rows_per_seq - 1) == 0
    pad_h = LANES - n_heads
    x2 = x.reshape(m, d)
    toep, pmat, qt, dec = _s5_params(s5_A_re[0], s5_A_im[0], s5_log_dt[0], s5_B_re[0], s5_B_im[0],
                                     s5_C_re[0], s5_C_im[0])
    wi = w_in[0]
    wut = wi[:, :d].T.astype(bf16)
    wr = jnp.pad(wi[:, d:], ((0, 0), (0, pad_h))).astype(bf16)
    u3, zs, zd, xbc, dtr = _in_proj(x2, norm_w[0][None, :], wut, wr, _block(m, 512))
    drow = jnp.repeat(s5_D[0].reshape(-1, S5_GROUP_CH), S5_CHUNK, axis=1)[:, None, :]
    y3 = _s5_mix(u3, toep, pmat, qt, dec, drow, rows_per_seq)
    head_of = jnp.arange(d) // SSD_HEAD_DIM
    e2 = (jnp.arange(LANES)[:, None] == head_of[None, :]).astype(bf16)
    e2 = jnp.concatenate([e2, e2], axis=0)
    pad1 = lambda v: jnp.pad(v, (0, pad_h))[None, :]
    yd = _ssd(xbc, dtr, zd, conv_w[0], conv_b[0][None, :], pad1(dt_bias[0]), pad1(A_log[0]),
              jnp.repeat(ssd_D[0], SSD_HEAD_DIM)[None, :], ssd_norm_w[0][None, :], e2, batch, _block(seq, 256))
    out = _out_proj(x2, y3, zs, yd, p[0].reshape(m, -1), s5_w_glu[0].astype(bf16), s5_b_glu[0][None, :],
                    w_out[0].astype(bf16), ple_norm_w[0][None, :], w_ple_gate[0].astype(bf16),
                    w_ple_proj[0].astype(bf16), final_norm_w[None, :], _block(m, 256))
    return out.reshape(batch, seq, d)
```

```python
import functools

import jax
import jax.numpy as jnp
from jax import lax
from jax.experimental import pallas as pl
from jax.experimental.pallas import tpu as pltpu

f32 = jnp.float32
bf16 = jnp.bfloat16

EPS = 1e-6
LANES = 128
S5_GROUP_CH = 16
S5_STATE = 64
S5_CHUNK = 64
SSD_HEAD_DIM = 64
SSD_GROUPS = 4
SSD_STATE = 128
SSD_CHUNK = 128
CONV_WIDTH = 4
CONV_ROWS, CONV_COLS = 32, 128
VMEM_LIMIT = 56 * 1024 * 1024


def _cparams(*sem):
    return pltpu.CompilerParams(dimension_semantics=sem, vmem_limit_bytes=VMEM_LIMIT)


def _const_spec(shape):
    nd = len(shape)
    return pl.BlockSpec(shape, lambda *_: (0,) * nd, pipeline_mode=pl.Buffered(1))


def _swap_halves(x):
    return pltpu.roll(x, LANES // 2, x.ndim - 1)


def _s5_params_kernel(are_ref, aim_ref, ldt_ref, bt_ref, ca_ref, cr_ref, ci_ref,
                      toep_ref, p_ref, qt_ref, dec_ref, kv_ref):
    T, H = S5_CHUNK, S5_GROUP_CH
    ar2, ai2 = are_ref[0], aim_ref[0]
    dt = jnp.exp(ldt_ref[0])
    xr, xi = ar2 * dt, ai2 * dt
    lane1 = lax.broadcasted_iota(jnp.int32, (1, LANES), 1)
    sgn = jnp.where(lane1 < T, -1.0, 1.0).astype(f32)

    def cmul(dr, di, x):
        return dr * x + di * (sgn * _swap_halves(x))

    ea = jnp.exp(xr)
    abr, abi = ea * jnp.cos(xi), ea * jnp.sin(xi)
    den = ar2 * ar2 + ai2 * ai2
    nr, ni = abr - 1.0, abi
    qr = (nr * ar2 + ni * ai2) / den
    qi = (ni * ar2 - nr * ai2) / den
    bb = cmul(qr, qi, bt_ref[0])

    tau = lax.broadcasted_iota(jnp.int32, (T, LANES), 0).astype(f32)

    def powtab(t):
        e = jnp.exp(t * xr)
        return e * jnp.cos(t * xi), e * jnp.sin(t * xi)

    pr_rev, pi_rev = powtab((T - 1.0) - tau)
    pr1, pi1 = powtab(tau + 1.0)
    pr0, pi0 = powtab(tau)
    conj = -sgn
    for h in range(H):
        p_ref[0, h * T:(h + 1) * T, :] = cmul(pr_rev, pi_rev, bb[h:h + 1, :]).astype(p_ref.dtype)
        qt_ref[0, h * T:(h + 1) * T, :] = (conj * cmul(pr1, pi1, ca_ref[0, h:h + 1, :])).astype(qt_ref.dtype)

    pr_t, pi_t = powtab(jnp.full((1, LANES), float(T), f32))
    dec_ref[0] = jnp.zeros((8, LANES), f32)
    dec_ref[0, 0:1, :] = pr_t
    dec_ref[0, 1:2, :] = sgn * pi_t

    pows = jnp.where(lax.broadcasted_iota(jnp.int32, (T, LANES), 1) < T, pr0, -pi0)
    zeros = jnp.zeros((T, LANES), f32)
    rhs = jnp.concatenate([jnp.concatenate([pows, zeros], axis=1),
                           jnp.concatenate([zeros, pows], axis=1)], axis=0)
    cb = [cr_ref[0, hp:hp + 1, :] * bb + ci_ref[0, hp:hp + 1, :] * (sgn * _swap_halves(bb)) for hp in range(H)]
    lhs = jnp.concatenate([jnp.concatenate([cb[2 * m], cb[2 * m + 1]], axis=1) for m in range(H // 2)], axis=0)
    kv_ref[...] = lax.dot_general(lhs, rhs, (((1,), (1,)), ((), ())), precision=lax.Precision.HIGHEST,
                                  preferred_element_type=f32)

    row = lax.broadcasted_iota(jnp.int32, (T, LANES), 0)
    col = lax.broadcasted_iota(jnp.int32, (T, LANES), 1)
    causal = (col % T) >= row
    for h in range(H):
        for m in range(H // 2):
            v = jnp.broadcast_to(kv_ref[m * H + h:m * H + h + 1, :], (T, LANES))
            tile = pltpu.roll(v, 0, 1, stride=1, stride_axis=0)
            toep_ref[0, h * T:(h + 1) * T, m * LANES:(m + 1) * LANES] = jnp.where(causal, tile, 0.0).astype(toep_ref.dtype)


def _s5_params(s5_A_re, s5_A_im, s5_log_dt, s5_B_re, s5_B_im, s5_C_re, s5_C_im):
    G, N = s5_A_re.shape
    H, T = S5_GROUP_CH, S5_CHUNK
    dup = lambda a: jnp.concatenate([a, a], axis=-1)
    are = dup(s5_A_re)[:, None, :]
    aim = dup(s5_A_im)[:, None, :]
    ldt = jnp.broadcast_to(s5_log_dt[:, None, None], (G, 1, 2 * N))
    bt = jnp.concatenate([jnp.swapaxes(s5_B_re, 1, 2), jnp.swapaxes(s5_B_im, 1, 2)], axis=-1)
    ca = jnp.concatenate([s5_C_re, s5_C_im], axis=-1)
    cr, ci = dup(s5_C_re), dup(s5_C_im)
    g3 = lambda a: pl.BlockSpec((1,) + a.shape[1:], lambda g: (g, 0, 0))
    ins = (are, aim, ldt, bt, ca, cr, ci)
    return pl.pallas_call(
        _s5_params_kernel,
        out_shape=(jax.ShapeDtypeStruct((G, H * T, H * T), bf16),
                   jax.ShapeDtypeStruct((G, H * T, 2 * N), bf16),
                   jax.ShapeDtypeStruct((G, H * T, 2 * N), bf16),
                   jax.ShapeDtypeStruct((G, 8, 2 * N), f32)),
        grid=(G,),
        in_specs=[g3(a) for a in ins],
        out_specs=(pl.BlockSpec((1, H * T, H * T), lambda g: (g, 0, 0)),
                   pl.BlockSpec((1, H * T, 2 * N), lambda g: (g, 0, 0)),
                   pl.BlockSpec((1, H * T, 2 * N), lambda g: (g, 0, 0)),
                   pl.BlockSpec((1, 8, 2 * N), lambda g: (g, 0, 0))),
        scratch_shapes=[pltpu.VMEM((H // 2 * H, LANES), f32)],
        compiler_params=_cparams("arbitrary"),
        name="s5_params",
    )(*ins)


def _rms(x, w):
    return x * lax.rsqrt(jnp.mean(x * x, axis=-1, keepdims=True) + EPS) * w


def _conv_silu(xpad_ref, cw_ref, cb_ref, xc_ref, c_lo, c_hi):
    tb = xc_ref.shape[0]
    for c0 in range(c_lo, c_hi, CONV_COLS):
        cols = slice(c0, c0 + CONV_COLS)
        sub = lax.broadcasted_iota(jnp.int32, (8, CONV_COLS), 0)
        prev_rolled = [pltpu.roll(xpad_ref[0:8, cols], sh, 0) for sh in range(1, CONV_WIDTH)]
        for r0 in range(0, tb, CONV_ROWS):
            outs = []
            for r in range(r0, r0 + CONV_ROWS, 8):
                cur = xpad_ref[8 + r:16 + r, cols]
                acc = cb_ref[:, cols] + cw_ref[CONV_WIDTH - 1, :, cols] * cur
                cur_rolled = [pltpu.roll(cur, sh, 0) for sh in range(1, CONV_WIDTH)]
                for sh in range(1, CONV_WIDTH):
                    acc = acc + cw_ref[CONV_WIDTH - 1 - sh, :, cols] * jnp.where(sub < sh, prev_rolled[sh - 1],
                                                                                  cur_rolled[sh - 1])
                outs.append(acc * jax.nn.sigmoid(acc))
                prev_rolled = cur_rolled
            xc_ref[r0:r0 + CONV_ROWS, cols] = jnp.concatenate(outs, axis=0).astype(xc_ref.dtype)


def _in_proj_kernel(x_ref, nw_ref, wut_ref, wr_ref, cw_ref, cb_ref, dtb_ref,
                    u3_ref, zs_ref, zd_ref, xc_ref, dt_ref, xpad_ref, hn_ref, *, blocks_per_seq):
    tb, d = x_ref.shape
    i = pl.program_id(0)

    @pl.when(i == 0)
    def _():
        xpad_ref[tb:tb + 8, :] = jnp.zeros((8, xpad_ref.shape[1]), f32)

    @pl.when(i >= 0)
    def _():
        hn_ref[...] = _rms(x_ref[...], nw_ref[...]).astype(bf16)
        tail = xpad_ref[tb:tb + 8, :]
        xpad_ref[0:8, :] = jnp.where(i % blocks_per_seq != 0, tail, 0.0)

    hn = hn_ref[...]

    def z_cols(lo, hi, ref):
        ref[...] = jnp.dot(hn, wr_ref[:, lo:hi], preferred_element_type=f32).astype(ref.dtype)

    def dt_cols():
        dt_ref[...] = jax.nn.softplus(jnp.dot(hn, wr_ref[:, 4 * d:], preferred_element_type=f32) + dtb_ref[...])

    def u3_rows(jj):
        v = lax.dot_general(wut_ref[...], hn[jj * 2 * LANES:(jj + 1) * 2 * LANES, :],
                            (((1,), (1,)), ((), ())), preferred_element_type=f32)
        u3_ref[2 * jj] = v[:, :LANES]
        u3_ref[2 * jj + 1] = v[:, LANES:]

    n_slabs = 4
    slab = 2 * d // n_slabs
    for k in range(n_slabs):
        xpad_ref[8:8 + tb, k * slab:(k + 1) * slab] = jnp.dot(hn, wr_ref[:, k * slab:(k + 1) * slab],
                                                              preferred_element_type=f32)
    for jj in range(tb // (2 * LANES)):
        u3_rows(jj)
    for k in range(n_slabs):
        _conv_silu(xpad_ref, cw_ref, cb_ref, xc_ref, k * slab, (k + 1) * slab)
    z_cols(2 * d, 3 * d, zs_ref)
    z_cols(3 * d, 4 * d, zd_ref)
    dt_cols()


def _in_proj(x2, norm_w, wut, wr, cw, cb, dtb, tb, seq):
    m, d = x2.shape
    nr = wr.shape[1]
    tok = lambda i: (i, 0)
    return pl.pallas_call(
        functools.partial(_in_proj_kernel, blocks_per_seq=seq // tb),
        out_shape=(jax.ShapeDtypeStruct((m // LANES, d, LANES), f32),
                   jax.ShapeDtypeStruct((m, d), bf16),
                   jax.ShapeDtypeStruct((m, d), bf16),
                   jax.ShapeDtypeStruct((m, 2 * d), bf16),
                   jax.ShapeDtypeStruct((m, LANES), f32)),
        grid=(m // tb,),
        in_specs=[pl.BlockSpec((tb, d), tok),
                  _const_spec((1, d)), _const_spec((d, d)), _const_spec((d, nr)),
                  _const_spec(cw.shape), _const_spec(cb.shape), _const_spec(dtb.shape)],
        out_specs=(pl.BlockSpec((tb // LANES, d, LANES), lambda i: (i, 0, 0)),
                   pl.BlockSpec((tb, d), tok), pl.BlockSpec((tb, d), tok),
                   pl.BlockSpec((tb, 2 * d), tok), pl.BlockSpec((tb, LANES), tok)),
        scratch_shapes=[pltpu.VMEM((tb + 8, 2 * d), f32), pltpu.VMEM((tb, d), bf16)],
        compiler_params=_cparams("arbitrary"),
        name="in_proj",
    )(x2, norm_w, wut, wr, cw, cb, dtb)


def _s5_mix_kernel(u_ref, toep_ref, p_ref, qt_ref, dec_ref, d_ref, o_ref, *, rows_per_seq):
    J = u_ref.shape[0]
    H = S5_GROUP_CH
    half = LANES // 2
    lo = lax.broadcasted_iota(jnp.int32, (J, LANES), 1) < half
    e0, e1 = [], []
    for m in range(H // 2):
        a, b = u_ref[:, 2 * m, :], u_ref[:, 2 * m + 1, :]
        e0.append(jnp.where(lo, a, _swap_halves(b)))
        e1.append(jnp.where(lo, _swap_halves(a), b))
    e = jnp.concatenate([jnp.concatenate(e0, axis=1), jnp.concatenate(e1, axis=1)], axis=0)
    eb = e.astype(bf16)
    y = jnp.dot(eb, toep_ref[0], preferred_element_type=f32)
    s = jnp.dot(eb, p_ref[0], preferred_element_type=f32)
    s0, s1 = s[:J], s[J:]
    ar, ai = dec_ref[0, 0:1, :], dec_ref[0, 1:2, :]

    def cm(cr, ci, x):
        return cr * x + ci * _swap_halves(x)

    def csq(cr, ci):
        return cr * cr - ci * ci, 2.0 * cr * ci

    v = cm(ar, ai, s0) + s1
    k_idx = lax.broadcasted_iota(jnp.int32, (J, LANES), 0) % rows_per_seq
    w = v
    dr, di = csq(ar, ai)
    step = 1
    while step < rows_per_seq:
        sh = jnp.where(k_idx >= step, pltpu.roll(w, step, 0), 0.0)
        w = w + cm(dr, di, sh)
        dr, di = csq(dr, di)
        step *= 2
    h0 = jnp.where(k_idx >= 1, pltpu.roll(w, 1, 0), 0.0)
    h1 = cm(ar, ai, h0) + s0
    hb = jnp.concatenate([h0, h1], axis=0).astype(bf16)
    y = y + lax.dot_general(hb, qt_ref[0], (((1,), (1,)), ((), ())), preferred_element_type=f32)
    y = jax.nn.gelu(y + e * d_ref[0])
    for m in range(H // 2):
        y0 = y[:J, m * LANES:(m + 1) * LANES]
        y1 = y[J:, m * LANES:(m + 1) * LANES]
        o_ref[:, 2 * m, :] = jnp.where(lo, y0, _swap_halves(y1))
        o_ref[:, 2 * m + 1, :] = jnp.where(lo, _swap_halves(y0), y1)


def _s5_mix(u3, toep, p, qt, dec, drow, rows_per_seq):
    J, d, _ = u3.shape
    G = toep.shape[0]
    H = S5_GROUP_CH
    w = toep.shape[1]
    n2 = p.shape[2]
    return pl.pallas_call(
        functools.partial(_s5_mix_kernel, rows_per_seq=rows_per_seq),
        out_shape=jax.ShapeDtypeStruct((J, d, LANES), f32),
        grid=(G,),
        in_specs=[pl.BlockSpec((J, H, LANES), lambda g: (0, g, 0)),
                  pl.BlockSpec((1, w, w), lambda g: (g, 0, 0)),
                  pl.BlockSpec((1, w, n2), lambda g: (g, 0, 0)),
                  pl.BlockSpec((1, w, n2), lambda g: (g, 0, 0)),
                  pl.BlockSpec((1, 8, n2), lambda g: (g, 0, 0)),
                  pl.BlockSpec((1, 1, w), lambda g: (g, 0, 0))],
        out_specs=pl.BlockSpec((J, H, LANES), lambda g: (0, g, 0)),
        compiler_params=_cparams("parallel"),
        name="s5_mix",
    )(u3, toep, p, qt, dec, drow)


def _ssd_kernel(xc_ref, dt_ref, zd_ref, alog_ref, dexp_ref, nw_ref, e2_ref, o_ref, state_ref):
    ts = xc_ref.shape[0]
    d = o_ref.shape[1]
    T, N, P = SSD_CHUNK, SSD_STATE, SSD_HEAD_DIM
    gw = d // SSD_GROUPS
    hpg = gw // P

    @pl.when(pl.program_id(1) == 0)
    def _():
        state_ref[...] = jnp.zeros_like(state_ref)

    dt = dt_ref[...]
    a = dt * (-jnp.exp(alog_ref[...]))
    e2 = e2_ref[...]

    def expand(v):
        hi = v.astype(bf16)
        lo = (v - hi.astype(f32)).astype(bf16)
        return jnp.dot(jnp.concatenate([hi, lo], axis=1), e2, preferred_element_type=f32)

    li = lax.broadcasted_iota(jnp.int32, (T, T), 0)
    si = lax.broadcasted_iota(jnp.int32, (T, T), 1)
    causal = li >= si
    tri = causal.astype(f32)
    for c in range(ts // T):
        sl = slice(c * T, (c + 1) * T)
        a_cs = jnp.dot(tri, a[sl], precision=lax.Precision.HIGHEST, preferred_element_type=f32)
        a_last = a_cs[T - 1:T, :]
        a_cst = a_cs.T
        dt_x = expand(dt[sl])
        w_x = expand(jnp.exp(a_last - a_cs))
        e_x = expand(jnp.exp(a_cs))
        cd_x = expand(jnp.broadcast_to(jnp.exp(a_last), (8, LANES)))[0:1, :]
        x_c = xc_ref[sl, :d].astype(f32)
        xd = x_c * dt_x
        xdb = xd.astype(bf16)
        xwb = (xd * w_x).astype(bf16)
        ys = []
        for q in range(SSD_GROUPS):
            bq = xc_ref[sl, d + q * N:d + (q + 1) * N]
            cq = xc_ref[sl, d + SSD_GROUPS * N + q * N:d + SSD_GROUPS * N + (q + 1) * N]
            g = lax.dot_general(cq, bq, (((1,), (1,)), ((), ())), preferred_element_type=f32)
            yd = []
            for hh in range(hpg):
                h = q * hpg + hh
                diff = a_cs[:, h:h + 1] - a_cst[h:h + 1, :]
                lm = jnp.exp(jnp.where(causal, diff, -1e30))
                sc = (g * lm).astype(bf16)
                yd.append(jnp.dot(sc, xdb[:, h * P:(h + 1) * P], preferred_element_type=f32))
            cs = slice(q * gw, (q + 1) * gw)
            st = state_ref[:, cs]
            y_off = jnp.dot(cq, st.astype(bf16), preferred_element_type=f32) * e_x[:, cs]
            s_new = lax.dot_general(bq, xwb[:, cs], (((0,), (0,)), ((), ())), preferred_element_type=f32)
            state_ref[:, cs] = st * cd_x[:, cs] + s_new
            ys.append(jnp.concatenate(yd, axis=1) + y_off)
        y = jnp.concatenate(ys, axis=1) + dexp_ref[...] * x_c
        z = zd_ref[sl, :].astype(f32)
        gt = y * (z * jax.nn.sigmoid(z))
        gn = []
        for q in range(SSD_GROUPS):
            gq = gt[:, q * gw:(q + 1) * gw]
            gn.append(gq * lax.rsqrt(jnp.sum(gq * gq, axis=-1, keepdims=True) * (1.0 / gw) + EPS))
        o_ref[sl, :] = (jnp.concatenate(gn, axis=1) * nw_ref[...]).astype(o_ref.dtype)


def _ssd(xc, dt, zd, alog, dexp, nw, e2, batch, ts):
    m, cdim = xc.shape
    d = zd.shape[1]
    nb = m // batch // ts
    tok = lambda b, i: (b * nb + i, 0)
    return pl.pallas_call(
        _ssd_kernel,
        out_shape=jax.ShapeDtypeStruct((m, d), bf16),
        grid=(batch, nb),
        in_specs=[pl.BlockSpec((ts, cdim), tok), pl.BlockSpec((ts, LANES), tok), pl.BlockSpec((ts, d), tok),
                  _const_spec(alog.shape), _const_spec(dexp.shape), _const_spec(nw.shape), _const_spec(e2.shape)],
        out_specs=pl.BlockSpec((ts, d), tok),
        scratch_shapes=[pltpu.VMEM((SSD_STATE, d), f32)],
        compiler_params=_cparams("arbitrary", "arbitrary"),
        name="ssd",
    )(xc, dt, zd, alog, dexp, nw, e2)


def _out_kernel(x_ref, y3_ref, zs_ref, yd_ref, p_ref, wglu_ref, bglu_ref, wo_ref, pnw_ref, wg_ref, wp_ref, fnw_ref,
                o_ref):
    d = x_ref.shape[1]
    y = jnp.concatenate([y3_ref[jj].T for jj in range(y3_ref.shape[0])], axis=0)
    glu = y * jax.nn.sigmoid(jnp.dot(y.astype(bf16), wglu_ref[...], preferred_element_type=f32) + bglu_ref[...])
    z = zs_ref[...].astype(f32)
    ys5 = glu * (z * jax.nn.sigmoid(z))
    h = x_ref[...] + jnp.dot(ys5.astype(bf16), wo_ref[:d, :], preferred_element_type=f32)
    h = h + jnp.dot(yd_ref[...], wo_ref[d:, :], preferred_element_type=f32)
    gate = jax.nn.sigmoid(jnp.dot(_rms(h, pnw_ref[...]).astype(bf16), wg_ref[...], preferred_element_type=f32))
    h = h + jnp.dot(p_ref[...].astype(bf16), wp_ref[...], preferred_element_type=f32) * gate
    o_ref[...] = _rms(h, fnw_ref[...])


def _out_proj(x2, y3, zs, yd, p2, wglu, bglu, wo, pnw, wg, wp, fnw, tb):
    m, d = x2.shape
    tok = lambda i: (i, 0)
    return pl.pallas_call(
        _out_kernel,
        out_shape=jax.ShapeDtypeStruct((m, d), f32),
        grid=(m // tb,),
        in_specs=[pl.BlockSpec((tb, d), tok),
                  pl.BlockSpec((tb // LANES, d, LANES), lambda i: (i, 0, 0)),
                  pl.BlockSpec((tb, d), tok), pl.BlockSpec((tb, d), tok), pl.BlockSpec((tb, p2.shape[1]), tok),
                  _const_spec(wglu.shape), _const_spec(bglu.shape), _const_spec(wo.shape), _const_spec(pnw.shape),
                  _const_spec(wg.shape), _const_spec(wp.shape), _const_spec(fnw.shape)],
        out_specs=pl.BlockSpec((tb, d), tok),
        compiler_params=_cparams("parallel"),
        name="out_proj",
    )(x2, y3, zs, yd, p2, wglu, bglu, wo, pnw, wg, wp, fnw)


def _block(m, want):
    b = want
    while m % b:
        b //= 2
    return b


def kernel(x, p, norm_w, w_in, s5_A_re, s5_A_im, s5_log_dt, s5_B_re, s5_B_im, s5_C_re, s5_C_im, s5_D, s5_w_glu, s5_b_glu, conv_w, conv_b, dt_bias, A_log, ssd_D, ssd_norm_w, w_out, ple_norm_w, w_ple_gate, w_ple_proj, final_norm_w):
    batch, seq, d = x.shape
    assert norm_w.shape[0] == 1, "single-layer problem: the final norm is fused into the layer's last kernel"
    m = batch * seq
    n_heads = dt_bias.shape[1]
    assert seq % (2 * LANES) == 0 and d % LANES == 0
    rows_per_seq = seq // LANES
    assert rows_per_seq & (rows_per_seq - 1) == 0
    pad_h = LANES - n_heads
    x2 = x.reshape(m, d)
    toep, pmat, qt, dec = _s5_params(s5_A_re[0], s5_A_im[0], s5_log_dt[0], s5_B_re[0], s5_B_im[0],
                                     s5_C_re[0], s5_C_im[0])
    wi = w_in[0]
    wut = wi[:, :d].T.astype(bf16)
    wr = jnp.concatenate([wi[:, 3 * d:5 * d], wi[:, d:3 * d], jnp.pad(wi[:, 5 * d:], ((0, 0), (0, pad_h)))],
                         axis=1).astype(bf16)
    pad1 = lambda v: jnp.pad(v, (0, pad_h))[None, :]
    cw8 = jnp.broadcast_to(conv_w[0][:, None, :], (CONV_WIDTH, 8, conv_w.shape[2]))
    cb8 = jnp.broadcast_to(conv_b[0][None, :], (8, conv_b.shape[1]))
    u3, zs, zd, xc, dt = _in_proj(x2, norm_w[0][None, :], wut, wr, cw8, cb8, pad1(dt_bias[0]), _block(seq, 512), seq)
    drow = jnp.repeat(s5_D[0].reshape(-1, S5_GROUP_CH), S5_CHUNK, axis=1)[:, None, :]
    y3 = _s5_mix(u3, toep, pmat, qt, dec, drow, rows_per_seq)
    head_of = jnp.arange(d) // SSD_HEAD_DIM
    e2 = (jnp.arange(LANES)[:, None] == head_of[None, :]).astype(bf16)
    e2 = jnp.concatenate([e2, e2], axis=0)
    yd = _ssd(xc, dt, zd, pad1(A_log[0]), jnp.repeat(ssd_D[0], SSD_HEAD_DIM)[None, :], ssd_norm_w[0][None, :], e2,
              batch, _block(seq, 256))
    out = _out_proj(x2, y3, zs, yd, p[0].reshape(m, -1), s5_w_glu[0].astype(bf16), s5_b_glu[0][None, :],
                    w_out[0].astype(bf16), ple_norm_w[0][None, :], w_ple_gate[0].astype(bf16),
                    w_ple_proj[0].astype(bf16), final_norm_w[None, :], _block(m, 256))
    return out.reshape(batch, seq, d)
```

```python
import functools

import jax
import jax.numpy as jnp
from jax import lax
from jax.experimental import pallas as pl
from jax.experimental.pallas import tpu as pltpu

f32 = jnp.float32
bf16 = jnp.bfloat16

EPS = 1e-6
LANES = 128
S5_GROUP_CH = 16
S5_STATE = 64
S5_CHUNK = 64
SSD_HEAD_DIM = 64
SSD_GROUPS = 4
SSD_STATE = 128
SSD_CHUNK = 128
CONV_WIDTH = 4
CONV_ROWS, CONV_COLS = 32, 128
VMEM_LIMIT = 56 * 1024 * 1024


def _cparams(*sem):
    return pltpu.CompilerParams(dimension_semantics=sem, vmem_limit_bytes=VMEM_LIMIT)


def _const_spec(shape):
    nd = len(shape)
    return pl.BlockSpec(shape, lambda *_: (0,) * nd, pipeline_mode=pl.Buffered(1))


def _swap_halves(x):
    return pltpu.roll(x, LANES // 2, x.ndim - 1)


def _s5_params_kernel(are_ref, aim_ref, ldt_ref, bt_ref, ca_ref, cr_ref, ci_ref,
                      toep_ref, p_ref, qt_ref, dec_ref, kv_ref):
    T, H = S5_CHUNK, S5_GROUP_CH
    ar2, ai2 = are_ref[0], aim_ref[0]
    dt = jnp.exp(ldt_ref[0])
    xr, xi = ar2 * dt, ai2 * dt
    lane1 = lax.broadcasted_iota(jnp.int32, (1, LANES), 1)
    sgn = jnp.where(lane1 < T, -1.0, 1.0).astype(f32)

    def cmul(dr, di, x):
        return dr * x + di * (sgn * _swap_halves(x))

    ea = jnp.exp(xr)
    abr, abi = ea * jnp.cos(xi), ea * jnp.sin(xi)
    den = ar2 * ar2 + ai2 * ai2
    nr, ni = abr - 1.0, abi
    qr = (nr * ar2 + ni * ai2) / den
    qi = (ni * ar2 - nr * ai2) / den
    bb = cmul(qr, qi, bt_ref[0])

    tau = lax.broadcasted_iota(jnp.int32, (T, LANES), 0).astype(f32)

    def powtab(t):
        e = jnp.exp(t * xr)
        return e * jnp.cos(t * xi), e * jnp.sin(t * xi)

    pr_rev, pi_rev = powtab((T - 1.0) - tau)
    pr1, pi1 = powtab(tau + 1.0)
    pr0, pi0 = powtab(tau)
    conj = -sgn
    for h in range(H):
        p_ref[0, h * T:(h + 1) * T, :] = cmul(pr_rev, pi_rev, bb[h:h + 1, :]).astype(p_ref.dtype)
        qt_ref[0, h * T:(h + 1) * T, :] = (conj * cmul(pr1, pi1, ca_ref[0, h:h + 1, :])).astype(qt_ref.dtype)

    pr_t, pi_t = powtab(jnp.full((1, LANES), float(T), f32))
    dec_ref[0] = jnp.zeros((8, LANES), f32)
    dec_ref[0, 0:1, :] = pr_t
    dec_ref[0, 1:2, :] = sgn * pi_t

    pows = jnp.where(lax.broadcasted_iota(jnp.int32, (T, LANES), 1) < T, pr0, -pi0)
    zeros = jnp.zeros((T, LANES), f32)
    rhs = jnp.concatenate([jnp.concatenate([pows, zeros], axis=1),
                           jnp.concatenate([zeros, pows], axis=1)], axis=0)
    cb = [cr_ref[0, hp:hp + 1, :] * bb + ci_ref[0, hp:hp + 1, :] * (sgn * _swap_halves(bb)) for hp in range(H)]
    lhs = jnp.concatenate([jnp.concatenate([cb[2 * m], cb[2 * m + 1]], axis=1) for m in range(H // 2)], axis=0)
    kv_ref[...] = lax.dot_general(lhs, rhs, (((1,), (1,)), ((), ())), precision=lax.Precision.HIGHEST,
                                  preferred_element_type=f32)

    row = lax.broadcasted_iota(jnp.int32, (T, LANES), 0)
    col = lax.broadcasted_iota(jnp.int32, (T, LANES), 1)
    causal = (col % T) >= row
    for h in range(H):
        for m in range(H // 2):
            v = jnp.broadcast_to(kv_ref[m * H + h:m * H + h + 1, :], (T, LANES))
            tile = pltpu.roll(v, 0, 1, stride=1, stride_axis=0)
            toep_ref[0, h * T:(h + 1) * T, m * LANES:(m + 1) * LANES] = jnp.where(causal, tile, 0.0).astype(toep_ref.dtype)


def _s5_params(s5_A_re, s5_A_im, s5_log_dt, s5_B_re, s5_B_im, s5_C_re, s5_C_im):
    G, N = s5_A_re.shape
    H, T = S5_GROUP_CH, S5_CHUNK
    dup = lambda a: jnp.concatenate([a, a], axis=-1)
    are = dup(s5_A_re)[:, None, :]
    aim = dup(s5_A_im)[:, None, :]
    ldt = jnp.broadcast_to(s5_log_dt[:, None, None], (G, 1, 2 * N))
    bt = jnp.concatenate([jnp.swapaxes(s5_B_re, 1, 2), jnp.swapaxes(s5_B_im, 1, 2)], axis=-1)
    ca = jnp.concatenate([s5_C_re, s5_C_im], axis=-1)
    cr, ci = dup(s5_C_re), dup(s5_C_im)
    g3 = lambda a: pl.BlockSpec((1,) + a.shape[1:], lambda g: (g, 0, 0))
    ins = (are, aim, ldt, bt, ca, cr, ci)
    return pl.pallas_call(
        _s5_params_kernel,
        out_shape=(jax.ShapeDtypeStruct((G, H * T, H * T), bf16),
                   jax.ShapeDtypeStruct((G, H * T, 2 * N), bf16),
                   jax.ShapeDtypeStruct((G, H * T, 2 * N), bf16),
                   jax.ShapeDtypeStruct((G, 8, 2 * N), f32)),
        grid=(G,),
        in_specs=[g3(a) for a in ins],
        out_specs=(pl.BlockSpec((1, H * T, H * T), lambda g: (g, 0, 0)),
                   pl.BlockSpec((1, H * T, 2 * N), lambda g: (g, 0, 0)),
                   pl.BlockSpec((1, H * T, 2 * N), lambda g: (g, 0, 0)),
                   pl.BlockSpec((1, 8, 2 * N), lambda g: (g, 0, 0))),
        scratch_shapes=[pltpu.VMEM((H // 2 * H, LANES), f32)],
        compiler_params=_cparams("arbitrary"),
        name="s5_params",
    )(*ins)


def _rms(x, w):
    return x * lax.rsqrt(jnp.mean(x * x, axis=-1, keepdims=True) + EPS) * w


def _conv_silu(xpad_ref, cw_ref, cb_ref, xc_ref, c_lo, c_hi):
    tb = xc_ref.shape[0]
    for c0 in range(c_lo, c_hi, CONV_COLS):
        cols = slice(c0, c0 + CONV_COLS)
        sub = lax.broadcasted_iota(jnp.int32, (8, CONV_COLS), 0)
        prev_rolled = [pltpu.roll(xpad_ref[0:8, cols], sh, 0) for sh in range(1, CONV_WIDTH)]
        for r0 in range(0, tb, CONV_ROWS):
            outs = []
            for r in range(r0, r0 + CONV_ROWS, 8):
                cur = xpad_ref[8 + r:16 + r, cols]
                acc = cb_ref[:, cols] + cw_ref[CONV_WIDTH - 1, :, cols] * cur
                cur_rolled = [pltpu.roll(cur, sh, 0) for sh in range(1, CONV_WIDTH)]
                for sh in range(1, CONV_WIDTH):
                    acc = acc + cw_ref[CONV_WIDTH - 1 - sh, :, cols] * jnp.where(sub < sh, prev_rolled[sh - 1],
                                                                                  cur_rolled[sh - 1])
                outs.append(acc * jax.nn.sigmoid(acc))
                prev_rolled = cur_rolled
            xc_ref[r0:r0 + CONV_ROWS, cols] = jnp.concatenate(outs, axis=0).astype(xc_ref.dtype)


def _in_proj_kernel(x_ref, nw_ref, wut_ref, wr_ref, cw_ref, cb_ref, dtb_ref,
                    u3_ref, zs_ref, zd_ref, xc_ref, dt_ref, xpad_ref, hn_ref, *, blocks_per_seq):
    tb, d = x_ref.shape
    i = pl.program_id(0)

    @pl.when(i == 0)
    def _():
        xpad_ref[tb:tb + 8, :] = jnp.zeros((8, xpad_ref.shape[1]), f32)

    @pl.when(i >= 0)
    def _():
        hn_ref[...] = _rms(x_ref[...], nw_ref[...]).astype(bf16)
        tail = xpad_ref[tb:tb + 8, :]
        xpad_ref[0:8, :] = jnp.where(i % blocks_per_seq != 0, tail, 0.0)

    hn = hn_ref[...]

    def z_cols(lo, hi, ref):
        ref[...] = jnp.dot(hn, wr_ref[:, lo:hi], preferred_element_type=f32).astype(ref.dtype)

    def dt_cols():
        dt_ref[...] = jax.nn.softplus(jnp.dot(hn, wr_ref[:, 4 * d:], preferred_element_type=f32) + dtb_ref[...])

    def u3_rows(jj):
        v = lax.dot_general(wut_ref[...], hn[jj * 2 * LANES:(jj + 1) * 2 * LANES, :],
                            (((1,), (1,)), ((), ())), preferred_element_type=f32)
        g = u3_ref.shape[2]
        for h in range(u3_ref.shape[0]):
            u3_ref[h, 2 * jj] = v[h * g:(h + 1) * g, :LANES]
            u3_ref[h, 2 * jj + 1] = v[h * g:(h + 1) * g, LANES:]

    n_slabs = 4
    slab = 2 * d // n_slabs
    for k in range(n_slabs):
        xpad_ref[8:8 + tb, k * slab:(k + 1) * slab] = jnp.dot(hn, wr_ref[:, k * slab:(k + 1) * slab],
                                                              preferred_element_type=f32)
    for jj in range(tb // (2 * LANES)):
        u3_rows(jj)
    for k in range(n_slabs):
        _conv_silu(xpad_ref, cw_ref, cb_ref, xc_ref, k * slab, (k + 1) * slab)
    z_cols(2 * d, 3 * d, zs_ref)
    z_cols(3 * d, 4 * d, zd_ref)
    dt_cols()


def _in_proj(x2, norm_w, wut, wr, cw, cb, dtb, tb, seq):
    m, d = x2.shape
    nr = wr.shape[1]
    tok = lambda i: (i, 0)
    return pl.pallas_call(
        functools.partial(_in_proj_kernel, blocks_per_seq=seq // tb),
        out_shape=(jax.ShapeDtypeStruct((S5_GROUP_CH, m // LANES, d // S5_GROUP_CH, LANES), f32),
                   jax.ShapeDtypeStruct((m, d), bf16),
                   jax.ShapeDtypeStruct((m, d), bf16),
                   jax.ShapeDtypeStruct((m, 2 * d), bf16),
                   jax.ShapeDtypeStruct((m, LANES), f32)),
        grid=(m // tb,),
        in_specs=[pl.BlockSpec((tb, d), tok),
                  _const_spec((1, d)), _const_spec((d, d)), _const_spec((d, nr)),
                  _const_spec(cw.shape), _const_spec(cb.shape), _const_spec(dtb.shape)],
        out_specs=(pl.BlockSpec((S5_GROUP_CH, tb // LANES, d // S5_GROUP_CH, LANES), lambda i: (0, i, 0, 0)),
                   pl.BlockSpec((tb, d), tok), pl.BlockSpec((tb, d), tok),
                   pl.BlockSpec((tb, 2 * d), tok), pl.BlockSpec((tb, LANES), tok)),
        scratch_shapes=[pltpu.VMEM((tb + 8, 2 * d), f32), pltpu.VMEM((tb, d), bf16)],
        compiler_params=_cparams("arbitrary"),
        name="in_proj",
    )(x2, norm_w, wut, wr, cw, cb, dtb)


def _s5_mix_kernel(u_hbm, toep_ref, p_ref, qt_ref, dec_ref, d_ref, o_hbm, ubuf, ybuf, sem_in, sem_out, *,
                   rows_per_seq):
    g = pl.program_id(0)
    n_g = pl.num_programs(0)
    slot = g % 2

    def in_copy(grp, s):
        return pltpu.make_async_copy(u_hbm.at[:, :, grp, :], ubuf.at[s], sem_in.at[s])

    def out_copy(grp, s):
        return pltpu.make_async_copy(ybuf.at[s], o_hbm.at[:, :, grp, :], sem_out.at[s])

    @pl.when(g == 0)
    def _():
        in_copy(0, 0).start()

    @pl.when(g + 1 < n_g)
    def _():
        in_copy(g + 1, 1 - slot).start()

    in_copy(g, slot).wait()

    @pl.when(g >= 2)
    def _():
        out_copy(g - 2, slot).wait()

    u_ref = ubuf.at[slot]
    o_ref = ybuf.at[slot]
    H, J, _ = u_ref.shape
    half = LANES // 2
    lo = lax.broadcasted_iota(jnp.int32, (J, LANES), 1) < half
    e0, e1 = [], []
    for m in range(H // 2):
        a, b = u_ref[2 * m], u_ref[2 * m + 1]
        e0.append(jnp.where(lo, a, _swap_halves(b)))
        e1.append(jnp.where(lo, _swap_halves(a), b))
    e = jnp.concatenate([jnp.concatenate(e0, axis=1), jnp.concatenate(e1, axis=1)], axis=0)
    eb = e.astype(bf16)
    y = jnp.dot(eb, toep_ref[0], preferred_element_type=f32)
    s = jnp.dot(eb, p_ref[0], preferred_element_type=f32)
    s0, s1 = s[:J], s[J:]
    ar, ai = dec_ref[0, 0:1, :], dec_ref[0, 1:2, :]

    def cm(cr, ci, x):
        return cr * x + ci * _swap_halves(x)

    def csq(cr, ci):
        return cr * cr - ci * ci, 2.0 * cr * ci

    v = cm(ar, ai, s0) + s1
    k_idx = lax.broadcasted_iota(jnp.int32, (J, LANES), 0) % rows_per_seq
    w = v
    dr, di = csq(ar, ai)
    step = 1
    while step < rows_per_seq:
        sh = jnp.where(k_idx >= step, pltpu.roll(w, step, 0), 0.0)
        w = w + cm(dr, di, sh)
        dr, di = csq(dr, di)
        step *= 2
    h0 = jnp.where(k_idx >= 1, pltpu.roll(w, 1, 0), 0.0)
    h1 = cm(ar, ai, h0) + s0
    hb = jnp.concatenate([h0, h1], axis=0).astype(bf16)
    y = y + lax.dot_general(hb, qt_ref[0], (((1,), (1,)), ((), ())), preferred_element_type=f32)
    y = jax.nn.gelu(y + e * d_ref[0])
    for m in range(H // 2):
        y0 = y[:J, m * LANES:(m + 1) * LANES]
        y1 = y[J:, m * LANES:(m + 1) * LANES]
        o_ref[2 * m] = jnp.where(lo, y0, _swap_halves(y1))
        o_ref[2 * m + 1] = jnp.where(lo, _swap_halves(y0), y1)

    out_copy(g, slot).start()

    @pl.when(g == n_g - 1)
    def _():
        @pl.when(n_g >= 2)
        def _():
            out_copy(g - 1, 1 - slot).wait()
        out_copy(g, slot).wait()


def _s5_mix(u3, toep, p, qt, dec, drow, rows_per_seq):
    H, J, G, _ = u3.shape
    w = toep.shape[1]
    n2 = p.shape[2]
    return pl.pallas_call(
        functools.partial(_s5_mix_kernel, rows_per_seq=rows_per_seq),
        out_shape=jax.ShapeDtypeStruct(u3.shape, f32),
        grid=(G,),
        in_specs=[pl.BlockSpec(memory_space=pl.ANY),
                  pl.BlockSpec((1, w, w), lambda g: (g, 0, 0)),
                  pl.BlockSpec((1, w, n2), lambda g: (g, 0, 0)),
                  pl.BlockSpec((1, w, n2), lambda g: (g, 0, 0)),
                  pl.BlockSpec((1, 8, n2), lambda g: (g, 0, 0)),
                  pl.BlockSpec((1, 1, w), lambda g: (g, 0, 0))],
        out_specs=pl.BlockSpec(memory_space=pl.ANY),
        scratch_shapes=[pltpu.VMEM((2, H, J, LANES), f32), pltpu.VMEM((2, H, J, LANES), f32),
                        pltpu.SemaphoreType.DMA((2,)), pltpu.SemaphoreType.DMA((2,))],
        compiler_params=_cparams("arbitrary"),
        name="s5_mix",
    )(u3, toep, p, qt, dec, drow)


def _ssd_kernel(xc_ref, dt_ref, zd_ref, alog_ref, dexp_ref, nw_ref, e2_ref, o_ref, state_ref):
    ts = xc_ref.shape[0]
    d = o_ref.shape[1]
    T, N, P = SSD_CHUNK, SSD_STATE, SSD_HEAD_DIM
    gw = d // SSD_GROUPS
    hpg = gw // P

    @pl.when(pl.program_id(1) == 0)
    def _():
        state_ref[...] = jnp.zeros_like(state_ref)

    dt = dt_ref[...]
    a = dt * (-jnp.exp(alog_ref[...]))
    e2 = e2_ref[...]

    def expand(v):
        hi = v.astype(bf16)
        lo = (v - hi.astype(f32)).astype(bf16)
        return jnp.dot(jnp.concatenate([hi, lo], axis=1), e2, preferred_element_type=f32)

    li = lax.broadcasted_iota(jnp.int32, (T, T), 0)
    si = lax.broadcasted_iota(jnp.int32, (T, T), 1)
    causal = li >= si
    tri = causal.astype(f32)
    for c in range(ts // T):
        sl = slice(c * T, (c + 1) * T)
        a_cs = jnp.dot(tri, a[sl], precision=lax.Precision.HIGHEST, preferred_element_type=f32)
        a_last = a_cs[T - 1:T, :]
        a_cst = a_cs.T
        dt_x = expand(dt[sl])
        w_x = expand(jnp.exp(a_last - a_cs))
        e_x = expand(jnp.exp(a_cs))
        cd_x = expand(jnp.broadcast_to(jnp.exp(a_last), (8, LANES)))[0:1, :]
        x_c = xc_ref[sl, :d].astype(f32)
        xd = x_c * dt_x
        xdb = xd.astype(bf16)
        xwb = (xd * w_x).astype(bf16)
        ys = []
        for q in range(SSD_GROUPS):
            bq = xc_ref[sl, d + q * N:d + (q + 1) * N]
            cq = xc_ref[sl, d + SSD_GROUPS * N + q * N:d + SSD_GROUPS * N + (q + 1) * N]
            g = lax.dot_general(cq, bq, (((1,), (1,)), ((), ())), preferred_element_type=f32)
            yd = []
            for hh in range(hpg):
                h = q * hpg + hh
                diff = a_cs[:, h:h + 1] - a_cst[h:h + 1, :]
                lm = jnp.exp(jnp.where(causal, diff, -1e30))
                sc = (g * lm).astype(bf16)
                yd.append(jnp.dot(sc, xdb[:, h * P:(h + 1) * P], preferred_element_type=f32))
            cs = slice(q * gw, (q + 1) * gw)
            st = state_ref[:, cs]
            y_off = jnp.dot(cq, st.astype(bf16), preferred_element_type=f32) * e_x[:, cs]
            s_new = lax.dot_general(bq, xwb[:, cs], (((0,), (0,)), ((), ())), preferred_element_type=f32)
            state_ref[:, cs] = st * cd_x[:, cs] + s_new
            ys.append(jnp.concatenate(yd, axis=1) + y_off)
        y = jnp.concatenate(ys, axis=1) + dexp_ref[...] * x_c
        z = zd_ref[sl, :].astype(f32)
        gt = y * (z * jax.nn.sigmoid(z))
        gn = []
        for q in range(SSD_GROUPS):
            gq = gt[:, q * gw:(q + 1) * gw]
            gn.append(gq * lax.rsqrt(jnp.sum(gq * gq, axis=-1, keepdims=True) * (1.0 / gw) + EPS))
        o_ref[sl, :] = (jnp.concatenate(gn, axis=1) * nw_ref[...]).astype(o_ref.dtype)


def _ssd(xc, dt, zd, alog, dexp, nw, e2, batch, ts):
    m, cdim = xc.shape
    d = zd.shape[1]
    nb = m // batch // ts
    tok = lambda b, i: (b * nb + i, 0)
    return pl.pallas_call(
        _ssd_kernel,
        out_shape=jax.ShapeDtypeStruct((m, d), bf16),
        grid=(batch, nb),
        in_specs=[pl.BlockSpec((ts, cdim), tok), pl.BlockSpec((ts, LANES), tok), pl.BlockSpec((ts, d), tok),
                  _const_spec(alog.shape), _const_spec(dexp.shape), _const_spec(nw.shape), _const_spec(e2.shape)],
        out_specs=pl.BlockSpec((ts, d), tok),
        scratch_shapes=[pltpu.VMEM((SSD_STATE, d), f32)],
        compiler_params=_cparams("arbitrary", "arbitrary"),
        name="ssd",
    )(xc, dt, zd, alog, dexp, nw, e2)


def _out_kernel(x_ref, y3_ref, zs_ref, yd_ref, p_ref, wglu_ref, bglu_ref, wo_ref, pnw_ref, wg_ref, wp_ref, fnw_ref,
                o_ref):
    d = x_ref.shape[1]
    y = jnp.concatenate([jnp.concatenate([y3_ref[h, jj] for h in range(y3_ref.shape[0])], axis=0).T
                         for jj in range(y3_ref.shape[1])], axis=0)
    glu = y * jax.nn.sigmoid(jnp.dot(y.astype(bf16), wglu_ref[...], preferred_element_type=f32) + bglu_ref[...])
    z = zs_ref[...].astype(f32)
    ys5 = glu * (z * jax.nn.sigmoid(z))
    h = x_ref[...] + jnp.dot(ys5.astype(bf16), wo_ref[:d, :], preferred_element_type=f32)
    h = h + jnp.dot(yd_ref[...], wo_ref[d:, :], preferred_element_type=f32)
    gate = jax.nn.sigmoid(jnp.dot(_rms(h, pnw_ref[...]).astype(bf16), wg_ref[...], preferred_element_type=f32))
    h = h + jnp.dot(p_ref[...].astype(bf16), wp_ref[...], preferred_element_type=f32) * gate
    o_ref[...] = _rms(h, fnw_ref[...])


def _out_proj(x2, y3, zs, yd, p2, wglu, bglu, wo, pnw, wg, wp, fnw, tb):
    m, d = x2.shape
    tok = lambda i: (i, 0)
    return pl.pallas_call(
        _out_kernel,
        out_shape=jax.ShapeDtypeStruct((m, d), f32),
        grid=(m // tb,),
        in_specs=[pl.BlockSpec((tb, d), tok),
                  pl.BlockSpec((y3.shape[0], tb // LANES, y3.shape[2], LANES), lambda i: (0, i, 0, 0)),
                  pl.BlockSpec((tb, d), tok), pl.BlockSpec((tb, d), tok), pl.BlockSpec((tb, p2.shape[1]), tok),
                  _const_spec(wglu.shape), _const_spec(bglu.shape), _const_spec(wo.shape), _const_spec(pnw.shape),
                  _const_spec(wg.shape), _const_spec(wp.shape), _const_spec(fnw.shape)],
        out_specs=pl.BlockSpec((tb, d), tok),
        compiler_params=_cparams("parallel"),
        name="out_proj",
    )(x2, y3, zs, yd, p2, wglu, bglu, wo, pnw, wg, wp, fnw)


def _block(m, want):
    b = want
    while m % b:
        b //= 2
    return b


def kernel(x, p, norm_w, w_in, s5_A_re, s5_A_im, s5_log_dt, s5_B_re, s5_B_im, s5_C_re, s5_C_im, s5_D, s5_w_glu, s5_b_glu, conv_w, conv_b, dt_bias, A_log, ssd_D, ssd_norm_w, w_out, ple_norm_w, w_ple_gate, w_ple_proj, final_norm_w):
    batch, seq, d = x.shape
    assert norm_w.shape[0] == 1, "single-layer problem: the final norm is fused into the layer's last kernel"
    m = batch * seq
    n_heads = dt_bias.shape[1]
    assert seq % (2 * LANES) == 0 and d % LANES == 0
    rows_per_seq = seq // LANES
    assert rows_per_seq & (rows_per_seq - 1) == 0
    pad_h = LANES - n_heads
    x2 = x.reshape(m, d)
    toep, pmat, qt, dec = _s5_params(s5_A_re[0], s5_A_im[0], s5_log_dt[0], s5_B_re[0], s5_B_im[0],
                                     s5_C_re[0], s5_C_im[0])
    n_grp = d // S5_GROUP_CH
    cols_hg = lambda w: w.reshape(w.shape[0], n_grp, S5_GROUP_CH).swapaxes(1, 2).reshape(w.shape[0], d)
    rows_hg = lambda w: cols_hg(w.T).T
    wi = w_in[0]
    wut = rows_hg(wi[:, :d].T).astype(bf16)
    wr = jnp.concatenate([wi[:, 3 * d:5 * d], cols_hg(wi[:, d:2 * d]), wi[:, 2 * d:3 * d],
                          jnp.pad(wi[:, 5 * d:], ((0, 0), (0, pad_h)))],
                         axis=1).astype(bf16)
    pad1 = lambda v: jnp.pad(v, (0, pad_h))[None, :]
    cw8 = jnp.broadcast_to(conv_w[0][:, None, :], (CONV_WIDTH, 8, conv_w.shape[2]))
    cb8 = jnp.broadcast_to(conv_b[0][None, :], (8, conv_b.shape[1]))
    u3, zs, zd, xc, dt = _in_proj(x2, norm_w[0][None, :], wut, wr, cw8, cb8, pad1(dt_bias[0]), _block(seq, 512), seq)
    drow = jnp.repeat(s5_D[0].reshape(-1, S5_GROUP_CH), S5_CHUNK, axis=1)[:, None, :]
    y3 = _s5_mix(u3, toep, pmat, qt, dec, drow, rows_per_seq)
    head_of = jnp.arange(d) // SSD_HEAD_DIM
    e2 = (jnp.arange(LANES)[:, None] == head_of[None, :]).astype(bf16)
    e2 = jnp.concatenate([e2, e2], axis=0)
    yd = _ssd(xc, dt, zd, pad1(A_log[0]), jnp.repeat(ssd_D[0], SSD_HEAD_DIM)[None, :], ssd_norm_w[0][None, :], e2,
              batch, _block(seq, 512))
    wo = jnp.concatenate([rows_hg(w_out[0][:d]), w_out[0][d:]], axis=0).astype(bf16)
    out = _out_proj(x2, y3, zs, yd, p[0].reshape(m, -1), rows_hg(cols_hg(s5_w_glu[0])).astype(bf16),
                    cols_hg(s5_b_glu[0][None, :]), wo, ple_norm_w[0][None, :], w_ple_gate[0].astype(bf16),
                    w_ple_proj[0].astype(bf16), final_norm_w[None, :], _block(m, 512))
    return out.reshape(batch, seq, d)
```

```python
import functools

import jax
import jax.numpy as jnp
from jax import lax
from jax.experimental import pallas as pl
from jax.experimental.pallas import tpu as pltpu

f32 = jnp.float32
bf16 = jnp.bfloat16

EPS = 1e-6
LANES = 128
S5_GROUP_CH = 16
S5_STATE = 64
S5_CHUNK = 64
SSD_HEAD_DIM = 64
SSD_GROUPS = 4
SSD_STATE = 128
SSD_CHUNK = 128
CONV_WIDTH = 4
CONV_ROWS, CONV_COLS = 32, 128
VMEM_LIMIT = 56 * 1024 * 1024


def _cparams(*sem):
    return pltpu.CompilerParams(dimension_semantics=sem, vmem_limit_bytes=VMEM_LIMIT)


def _const_spec(shape):
    nd = len(shape)
    return pl.BlockSpec(shape, lambda *_: (0,) * nd, pipeline_mode=pl.Buffered(1))


def _swap_halves(x):
    return pltpu.roll(x, LANES // 2, x.ndim - 1)


def _s5_params_kernel(are_ref, aim_ref, ldt_ref, bt_ref, ca_ref, cr_ref, ci_ref,
                      toep_ref, p_ref, qt_ref, dec_ref, kv_ref):
    T, H = S5_CHUNK, S5_GROUP_CH
    ar2, ai2 = are_ref[0], aim_ref[0]
    dt = jnp.exp(ldt_ref[0])
    xr, xi = ar2 * dt, ai2 * dt
    lane1 = lax.broadcasted_iota(jnp.int32, (1, LANES), 1)
    sgn = jnp.where(lane1 < T, -1.0, 1.0).astype(f32)

    def cmul(dr, di, x):
        return dr * x + di * (sgn * _swap_halves(x))

    ea = jnp.exp(xr)
    abr, abi = ea * jnp.cos(xi), ea * jnp.sin(xi)
    den = ar2 * ar2 + ai2 * ai2
    nr, ni = abr - 1.0, abi
    qr = (nr * ar2 + ni * ai2) / den
    qi = (ni * ar2 - nr * ai2) / den
    bb = cmul(qr, qi, bt_ref[0])

    def powtab(t):
        e = jnp.exp(t * xr)
        return e * jnp.cos(t * xi), e * jnp.sin(t * xi)

    def zmul(ar_, ai_, br_, bi_):
        return ar_ * br_ - ai_ * bi_, ar_ * bi_ + ai_ * br_

    j8 = lax.broadcasted_iota(jnp.int32, (8, LANES), 0).astype(f32)
    s8r, s8i = powtab(jnp.full((1, LANES), 8.0, f32))
    up, down = [powtab(j8)], [powtab(7.0 - j8)]
    for _ in range(T // 8 - 1):
        up.append(zmul(*up[-1], s8r, s8i))
        down.append(zmul(*down[-1], s8r, s8i))
    pr0, pi0 = (jnp.concatenate([t[k] for t in up], axis=0) for k in range(2))
    pr_rev, pi_rev = (jnp.concatenate([t[k] for t in down[::-1]], axis=0) for k in range(2))
    pr1, pi1 = zmul(pr0, pi0, abr, abi)
    conj = -sgn
    for h in range(H):
        p_ref[0, h * T:(h + 1) * T, :] = cmul(pr_rev, pi_rev, bb[h:h + 1, :]).astype(p_ref.dtype)
        qt_ref[0, h * T:(h + 1) * T, :] = (conj * cmul(pr1, pi1, ca_ref[0, h:h + 1, :])).astype(qt_ref.dtype)

    pr_t, pi_t = s8r, s8i
    for _ in range((T // 8).bit_length() - 1):
        pr_t, pi_t = zmul(pr_t, pi_t, pr_t, pi_t)
    dec_ref[0] = jnp.zeros((8, LANES), f32)
    dec_ref[0, 0:1, :] = pr_t
    dec_ref[0, 1:2, :] = sgn * pi_t

    pows = jnp.where(lax.broadcasted_iota(jnp.int32, (T, LANES), 1) < T, pr0, -pi0)
    zeros = jnp.zeros((T, LANES), f32)
    rhs = jnp.concatenate([jnp.concatenate([pows, zeros], axis=1),
                           jnp.concatenate([zeros, pows], axis=1)], axis=0)
    cb = [cr_ref[0, hp:hp + 1, :] * bb + ci_ref[0, hp:hp + 1, :] * (sgn * _swap_halves(bb)) for hp in range(H)]
    lhs = jnp.concatenate([jnp.concatenate([cb[2 * m], cb[2 * m + 1]], axis=1) for m in range(H // 2)], axis=0)
    kv_ref[...] = lax.dot_general(lhs, rhs, (((1,), (1,)), ((), ())), precision=lax.Precision.HIGHEST,
                                  preferred_element_type=f32)

    row = lax.broadcasted_iota(jnp.int32, (T, LANES), 0)
    col = lax.broadcasted_iota(jnp.int32, (T, LANES), 1)
    causal = (col % T) >= row
    for h in range(H):
        for m in range(H // 2):
            v = jnp.broadcast_to(kv_ref[m * H + h:m * H + h + 1, :], (T, LANES))
            tile = pltpu.roll(v, 0, 1, stride=1, stride_axis=0)
            toep_ref[0, h * T:(h + 1) * T, m * LANES:(m + 1) * LANES] = jnp.where(causal, tile, 0.0).astype(toep_ref.dtype)


def _s5_params(s5_A_re, s5_A_im, s5_log_dt, s5_B_re, s5_B_im, s5_C_re, s5_C_im):
    G, N = s5_A_re.shape
    H, T = S5_GROUP_CH, S5_CHUNK
    dup = lambda a: jnp.concatenate([a, a], axis=-1)
    are = dup(s5_A_re)[:, None, :]
    aim = dup(s5_A_im)[:, None, :]
    ldt = jnp.broadcast_to(s5_log_dt[:, None, None], (G, 1, 2 * N))
    bt = jnp.concatenate([jnp.swapaxes(s5_B_re, 1, 2), jnp.swapaxes(s5_B_im, 1, 2)], axis=-1)
    ca = jnp.concatenate([s5_C_re, s5_C_im], axis=-1)
    cr, ci = dup(s5_C_re), dup(s5_C_im)
    g3 = lambda a: pl.BlockSpec((1,) + a.shape[1:], lambda g: (g, 0, 0))
    ins = (are, aim, ldt, bt, ca, cr, ci)
    return pl.pallas_call(
        _s5_params_kernel,
        out_shape=(jax.ShapeDtypeStruct((G, H * T, H * T), bf16),
                   jax.ShapeDtypeStruct((G, H * T, 2 * N), bf16),
                   jax.ShapeDtypeStruct((G, H * T, 2 * N), bf16),
                   jax.ShapeDtypeStruct((G, 8, 2 * N), f32)),
        grid=(G,),
        in_specs=[g3(a) for a in ins],
        out_specs=(pl.BlockSpec((1, H * T, H * T), lambda g: (g, 0, 0)),
                   pl.BlockSpec((1, H * T, 2 * N), lambda g: (g, 0, 0)),
                   pl.BlockSpec((1, H * T, 2 * N), lambda g: (g, 0, 0)),
                   pl.BlockSpec((1, 8, 2 * N), lambda g: (g, 0, 0))),
        scratch_shapes=[pltpu.VMEM((H // 2 * H, LANES), f32)],
        compiler_params=_cparams("arbitrary"),
        name="s5_params",
    )(*ins)


def _rms(x, w):
    return x * lax.rsqrt(jnp.mean(x * x, axis=-1, keepdims=True) + EPS) * w


def _conv_silu(xpad_ref, cw_ref, cb_ref, xc_ref, c_lo, c_hi):
    tb = xc_ref.shape[0]
    for c0 in range(c_lo, c_hi, CONV_COLS):
        cols = slice(c0, c0 + CONV_COLS)
        sub = lax.broadcasted_iota(jnp.int32, (8, CONV_COLS), 0)
        prev_rolled = [pltpu.roll(xpad_ref[0:8, cols], sh, 0) for sh in range(1, CONV_WIDTH)]
        for r0 in range(0, tb, CONV_ROWS):
            outs = []
            for r in range(r0, r0 + CONV_ROWS, 8):
                cur = xpad_ref[8 + r:16 + r, cols]
                acc = cb_ref[:, cols] + cw_ref[CONV_WIDTH - 1, :, cols] * cur
                cur_rolled = [pltpu.roll(cur, sh, 0) for sh in range(1, CONV_WIDTH)]
                for sh in range(1, CONV_WIDTH):
                    acc = acc + cw_ref[CONV_WIDTH - 1 - sh, :, cols] * jnp.where(sub < sh, prev_rolled[sh - 1],
                                                                                  cur_rolled[sh - 1])
                outs.append(acc * jax.nn.sigmoid(acc))
                prev_rolled = cur_rolled
            xc_ref[r0:r0 + CONV_ROWS, cols] = jnp.concatenate(outs, axis=0).astype(xc_ref.dtype)


def _in_proj_kernel(x_ref, nw_ref, wut_ref, wr_ref, cw_ref, cb_ref, dtb_ref,
                    u3_ref, zs_ref, zd_ref, xc_ref, dt_ref, xpad_ref, hn_ref, *, blocks_per_seq):
    tb, d = x_ref.shape
    i = pl.program_id(0)

    @pl.when(i == 0)
    def _():
        xpad_ref[tb:tb + 8, :] = jnp.zeros((8, xpad_ref.shape[1]), f32)

    @pl.when(i >= 0)
    def _():
        hn_ref[...] = _rms(x_ref[...], nw_ref[...]).astype(bf16)
        tail = xpad_ref[tb:tb + 8, :]
        xpad_ref[0:8, :] = jnp.where(i % blocks_per_seq != 0, tail, 0.0)

    hn = hn_ref[...]

    def z_cols(lo, hi, ref):
        ref[...] = jnp.dot(hn, wr_ref[:, lo:hi], preferred_element_type=f32).astype(ref.dtype)

    def dt_cols():
        dt_ref[...] = jax.nn.softplus(jnp.dot(hn, wr_ref[:, 4 * d:], preferred_element_type=f32) + dtb_ref[...])

    def u3_rows(jj):
        v = lax.dot_general(wut_ref[...], hn[jj * 2 * LANES:(jj + 1) * 2 * LANES, :],
                            (((1,), (1,)), ((), ())), preferred_element_type=f32)
        g = u3_ref.shape[2]
        for h in range(u3_ref.shape[0]):
            u3_ref[h, 2 * jj] = v[h * g:(h + 1) * g, :LANES]
            u3_ref[h, 2 * jj + 1] = v[h * g:(h + 1) * g, LANES:]

    n_slabs = 4
    slab = 2 * d // n_slabs
    for k in range(n_slabs):
        xpad_ref[8:8 + tb, k * slab:(k + 1) * slab] = jnp.dot(hn, wr_ref[:, k * slab:(k + 1) * slab],
                                                              preferred_element_type=f32)
    for jj in range(tb // (2 * LANES)):
        u3_rows(jj)
    for k in range(n_slabs):
        _conv_silu(xpad_ref, cw_ref, cb_ref, xc_ref, k * slab, (k + 1) * slab)
    z_cols(2 * d, 3 * d, zs_ref)
    z_cols(3 * d, 4 * d, zd_ref)
    dt_cols()


def _in_proj(x2, norm_w, wut, wr, cw, cb, dtb, tb, seq):
    m, d = x2.shape
    nr = wr.shape[1]
    tok = lambda i: (i, 0)
    return pl.pallas_call(
        functools.partial(_in_proj_kernel, blocks_per_seq=seq // tb),
        out_shape=(jax.ShapeDtypeStruct((S5_GROUP_CH, m // LANES, d // S5_GROUP_CH, LANES), f32),
                   jax.ShapeDtypeStruct((m, d), bf16),
                   jax.ShapeDtypeStruct((m, d), bf16),
                   jax.ShapeDtypeStruct((m, 2 * d), bf16),
                   jax.ShapeDtypeStruct((m, LANES), f32)),
        grid=(m // tb,),
        in_specs=[pl.BlockSpec((tb, d), tok),
                  _const_spec((1, d)), _const_spec((d, d)), _const_spec((d, nr)),
                  _const_spec(cw.shape), _const_spec(cb.shape), _const_spec(dtb.shape)],
        out_specs=(pl.BlockSpec((S5_GROUP_CH, tb // LANES, d // S5_GROUP_CH, LANES), lambda i: (0, i, 0, 0)),
                   pl.BlockSpec((tb, d), tok), pl.BlockSpec((tb, d), tok),
                   pl.BlockSpec((tb, 2 * d), tok), pl.BlockSpec((tb, LANES), tok)),
        scratch_shapes=[pltpu.VMEM((tb + 8, 2 * d), f32), pltpu.VMEM((tb, d), bf16)],
        compiler_params=_cparams("arbitrary"),
        name="in_proj",
    )(x2, norm_w, wut, wr, cw, cb, dtb)


def _s5_mix_kernel(u_hbm, toep_ref, p_ref, qt_ref, dec_ref, d_ref, o_hbm, ubuf, ybuf, sem_in, sem_out, *,
                   rows_per_seq):
    g = pl.program_id(0)
    n_g = pl.num_programs(0)
    slot = g % 2

    def in_copy(grp, s):
        return pltpu.make_async_copy(u_hbm.at[:, :, grp, :], ubuf.at[s], sem_in.at[s])

    def out_copy(grp, s):
        return pltpu.make_async_copy(ybuf.at[s], o_hbm.at[:, :, grp, :], sem_out.at[s])

    @pl.when(g == 0)
    def _():
        in_copy(0, 0).start()

    @pl.when(g + 1 < n_g)
    def _():
        in_copy(g + 1, 1 - slot).start()

    in_copy(g, slot).wait()

    @pl.when(g >= 2)
    def _():
        out_copy(g - 2, slot).wait()

    u_ref = ubuf.at[slot]
    o_ref = ybuf.at[slot]
    H, J, _ = u_ref.shape
    half = LANES // 2
    lo = lax.broadcasted_iota(jnp.int32, (J, LANES), 1) < half
    e0, e1 = [], []
    for m in range(H // 2):
        a, b = u_ref[2 * m], u_ref[2 * m + 1]
        e0.append(jnp.where(lo, a, _swap_halves(b)))
        e1.append(jnp.where(lo, _swap_halves(a), b))
    e = jnp.concatenate([jnp.concatenate(e0, axis=1), jnp.concatenate(e1, axis=1)], axis=0)
    eb = e.astype(bf16)
    y = jnp.dot(eb, toep_ref[0], preferred_element_type=f32)
    s = jnp.dot(eb, p_ref[0], preferred_element_type=f32)
    s0, s1 = s[:J], s[J:]
    ar, ai = dec_ref[0, 0:1, :], dec_ref[0, 1:2, :]

    def cm(cr, ci, x):
        return cr * x + ci * _swap_halves(x)

    def csq(cr, ci):
        return cr * cr - ci * ci, 2.0 * cr * ci

    v = cm(ar, ai, s0) + s1
    k_idx = lax.broadcasted_iota(jnp.int32, (J, LANES), 0) % rows_per_seq
    w = v
    dr, di = csq(ar, ai)
    step = 1
    while step < rows_per_seq:
        sh = jnp.where(k_idx >= step, pltpu.roll(w, step, 0), 0.0)
        w = w + cm(dr, di, sh)
        dr, di = csq(dr, di)
        step *= 2
    h0 = jnp.where(k_idx >= 1, pltpu.roll(w, 1, 0), 0.0)
    h1 = cm(ar, ai, h0) + s0
    hb = jnp.concatenate([h0, h1], axis=0).astype(bf16)
    y = y + lax.dot_general(hb, qt_ref[0], (((1,), (1,)), ((), ())), preferred_element_type=f32)
    y = jax.nn.gelu(y + e * d_ref[0])
    for m in range(H // 2):
        y0 = y[:J, m * LANES:(m + 1) * LANES]
        y1 = y[J:, m * LANES:(m + 1) * LANES]
        o_ref[2 * m] = jnp.where(lo, y0, _swap_halves(y1))
        o_ref[2 * m + 1] = jnp.where(lo, _swap_halves(y0), y1)

    out_copy(g, slot).start()

    @pl.when(g == n_g - 1)
    def _():
        @pl.when(n_g >= 2)
        def _():
            out_copy(g - 1, 1 - slot).wait()
        out_copy(g, slot).wait()


def _s5_mix(u3, toep, p, qt, dec, drow, rows_per_seq):
    H, J, G, _ = u3.shape
    w = toep.shape[1]
    n2 = p.shape[2]
    return pl.pallas_call(
        functools.partial(_s5_mix_kernel, rows_per_seq=rows_per_seq),
        out_shape=jax.ShapeDtypeStruct(u3.shape, f32),
        grid=(G,),
        in_specs=[pl.BlockSpec(memory_space=pl.ANY),
                  pl.BlockSpec((1, w, w), lambda g: (g, 0, 0)),
                  pl.BlockSpec((1, w, n2), lambda g: (g, 0, 0)),
                  pl.BlockSpec((1, w, n2), lambda g: (g, 0, 0)),
                  pl.BlockSpec((1, 8, n2), lambda g: (g, 0, 0)),
                  pl.BlockSpec((1, 1, w), lambda g: (g, 0, 0))],
        out_specs=pl.BlockSpec(memory_space=pl.ANY),
        scratch_shapes=[pltpu.VMEM((2, H, J, LANES), f32), pltpu.VMEM((2, H, J, LANES), f32),
                        pltpu.SemaphoreType.DMA((2,)), pltpu.SemaphoreType.DMA((2,))],
        compiler_params=_cparams("arbitrary"),
        name="s5_mix",
    )(u3, toep, p, qt, dec, drow)


def _split_bf16(v, terms):
    parts, r = [], v
    for _ in range(terms):
        part = r.astype(bf16)
        parts.append(part)
        r = r - part.astype(f32)
    return jnp.concatenate(parts, axis=1)


def _ssd_kernel(xc_ref, dt_ref, zd_ref, alog_ref, dexp_ref, nw_ref, e2_ref, sel3_ref, o_ref, state_ref):
    ts = xc_ref.shape[0]
    d = o_ref.shape[1]
    T, N, P = SSD_CHUNK, SSD_STATE, SSD_HEAD_DIM
    gw = d // SSD_GROUPS
    hpg = gw // P
    n_chunks = ts // T

    @pl.when(pl.program_id(1) == 0)
    def _():
        state_ref[...] = jnp.zeros_like(state_ref)

    dt = dt_ref[...]
    a = dt * (-jnp.exp(alog_ref[...]))
    li = lax.broadcasted_iota(jnp.int32, (T, T), 0)
    si = lax.broadcasted_iota(jnp.int32, (T, T), 1)
    causal = li >= si
    tri = causal.astype(f32)
    a_cs_c = [jnp.dot(tri, a[c * T:(c + 1) * T], precision=lax.Precision.HIGHEST, preferred_element_type=f32)
              for c in range(n_chunks)]
    a_cs = jnp.concatenate(a_cs_c, axis=0)
    a_last = jnp.concatenate([jnp.broadcast_to(v[T - 1:T, :], (T, LANES)) for v in a_cs_c], axis=0)
    w = jnp.exp(a_last - a_cs) * dt
    e = jnp.exp(a_cs)
    cd = jnp.concatenate([jnp.broadcast_to(jnp.exp(v[T - 1:T, :]), (8, LANES)) for v in a_cs_c], axis=0)
    ex = jnp.dot(_split_bf16(jnp.concatenate([w, e, cd], axis=0), 2), e2_ref[...], preferred_element_type=f32)
    w_x, e_x, cd_x = ex[:ts], ex[ts:2 * ts], ex[2 * ts:]
    colb = jnp.dot(_split_bf16(a_cs, 3), sel3_ref[...], preferred_element_type=f32)
    head_of_lane = lax.broadcasted_iota(jnp.int32, (T, gw), 1) // P
    for c in range(n_chunks):
        sl = slice(c * T, (c + 1) * T)
        a_cst = a_cs_c[c].T
        dtt = dt[sl].T
        ys = []
        for q in range(SSD_GROUPS):
            cs = slice(q * gw, (q + 1) * gw)
            bq = xc_ref[sl, d + q * N:d + (q + 1) * N]
            cq = xc_ref[sl, d + SSD_GROUPS * N + q * N:d + SSD_GROUPS * N + (q + 1) * N]
            xq = xc_ref[sl, cs]
            g = lax.dot_general(cq, bq, (((1,), (1,)), ((), ())), preferred_element_type=f32)
            scores, xbd = [], []
            for hh in range(hpg):
                h = q * hpg + hh
                diff = colb[sl, h * T:(h + 1) * T] - a_cst[h:h + 1, :]
                lm = jnp.exp(jnp.where(causal, diff, -1e30))
                scores.append((g * lm * dtt[h:h + 1, :]).astype(bf16))
                xbd.append(jnp.where(head_of_lane == hh, xq, jnp.zeros_like(xq)))
            y_diag = jnp.dot(jnp.concatenate(scores, axis=1), jnp.concatenate(xbd, axis=0),
                             preferred_element_type=f32)
            st = state_ref[:, cs]
            y_off = jnp.dot(cq, st.astype(bf16), preferred_element_type=f32) * e_x[sl, cs]
            xw = xq * w_x[sl, cs].astype(bf16)
            s_new = lax.dot_general(bq, xw, (((0,), (0,)), ((), ())), preferred_element_type=f32)
            state_ref[:, cs] = st * cd_x[8 * c:8 * c + 1, cs] + s_new
            ys.append(y_diag + y_off)
        y = jnp.concatenate(ys, axis=1) + dexp_ref[...] * xc_ref[sl, :d].astype(f32)
        z = zd_ref[sl, :].astype(f32)
        gt = y * (z * jax.nn.sigmoid(z))
        gn = []
        for q in range(SSD_GROUPS):
            gq = gt[:, q * gw:(q + 1) * gw]
            gn.append(gq * lax.rsqrt(jnp.sum(gq * gq, axis=-1, keepdims=True) * (1.0 / gw) + EPS))
        o_ref[sl, :] = (jnp.concatenate(gn, axis=1) * nw_ref[...]).astype(o_ref.dtype)


def _ssd(xc, dt, zd, alog, dexp, nw, e2, sel3, batch, ts):
    m, cdim = xc.shape
    d = zd.shape[1]
    nb = m // batch // ts
    tok = lambda b, i: (b * nb + i, 0)
    return pl.pallas_call(
        _ssd_kernel,
        out_shape=jax.ShapeDtypeStruct((m, d), bf16),
        grid=(batch, nb),
        in_specs=[pl.BlockSpec((ts, cdim), tok), pl.BlockSpec((ts, LANES), tok), pl.BlockSpec((ts, d), tok),
                  _const_spec(alog.shape), _const_spec(dexp.shape), _const_spec(nw.shape), _const_spec(e2.shape),
                  _const_spec(sel3.shape)],
        out_specs=pl.BlockSpec((ts, d), tok),
        scratch_shapes=[pltpu.VMEM((SSD_STATE, d), f32)],
        compiler_params=_cparams("arbitrary", "arbitrary"),
        name="ssd",
    )(xc, dt, zd, alog, dexp, nw, e2, sel3)


def _out_kernel(x_ref, y3_ref, zs_ref, yd_ref, p_ref, wglu_ref, bglu_ref, wo_ref, pnw_ref, wg_ref, wp_ref, fnw_ref,
                o_ref):
    d = x_ref.shape[1]
    y = jnp.concatenate([jnp.concatenate([y3_ref[h, jj] for h in range(y3_ref.shape[0])], axis=0).T
                         for jj in range(y3_ref.shape[1])], axis=0)
    glu = y * jax.nn.sigmoid(jnp.dot(y.astype(bf16), wglu_ref[...], preferred_element_type=f32) + bglu_ref[...])
    z = zs_ref[...].astype(f32)
    ys5 = glu * (z * jax.nn.sigmoid(z))
    h = x_ref[...] + jnp.dot(ys5.astype(bf16), wo_ref[:d, :], preferred_element_type=f32)
    h = h + jnp.dot(yd_ref[...], wo_ref[d:, :], preferred_element_type=f32)
    gate = jax.nn.sigmoid(jnp.dot(_rms(h, pnw_ref[...]).astype(bf16), wg_ref[...], preferred_element_type=f32))
    h = h + jnp.dot(p_ref[...].astype(bf16), wp_ref[...], preferred_element_type=f32) * gate
    o_ref[...] = _rms(h, fnw_ref[...])


def _out_proj(x2, y3, zs, yd, p2, wglu, bglu, wo, pnw, wg, wp, fnw, tb):
    m, d = x2.shape
    tok = lambda i: (i, 0)
    return pl.pallas_call(
        _out_kernel,
        out_shape=jax.ShapeDtypeStruct((m, d), f32),
        grid=(m // tb,),
        in_specs=[pl.BlockSpec((tb, d), tok),
                  pl.BlockSpec((y3.shape[0], tb // LANES, y3.shape[2], LANES), lambda i: (0, i, 0, 0)),
                  pl.BlockSpec((tb, d), tok), pl.BlockSpec((tb, d), tok), pl.BlockSpec((tb, p2.shape[1]), tok),
                  _const_spec(wglu.shape), _const_spec(bglu.shape), _const_spec(wo.shape), _const_spec(pnw.shape),
                  _const_spec(wg.shape), _const_spec(wp.shape), _const_spec(fnw.shape)],
        out_specs=pl.BlockSpec((tb, d), tok),
        compiler_params=_cparams("parallel"),
        name="out_proj",
    )(x2, y3, zs, yd, p2, wglu, bglu, wo, pnw, wg, wp, fnw)


def _block(m, want):
    b = want
    while m % b:
        b //= 2
    return b


def kernel(x, p, norm_w, w_in, s5_A_re, s5_A_im, s5_log_dt, s5_B_re, s5_B_im, s5_C_re, s5_C_im, s5_D, s5_w_glu, s5_b_glu, conv_w, conv_b, dt_bias, A_log, ssd_D, ssd_norm_w, w_out, ple_norm_w, w_ple_gate, w_ple_proj, final_norm_w):
    batch, seq, d = x.shape
    assert norm_w.shape[0] == 1, "single-layer problem: the final norm is fused into the layer's last kernel"
    m = batch * seq
    n_heads = dt_bias.shape[1]
    assert seq % (2 * LANES) == 0 and d % LANES == 0
    rows_per_seq = seq // LANES
    assert rows_per_seq & (rows_per_seq - 1) == 0
    pad_h = LANES - n_heads
    x2 = x.reshape(m, d)
    toep, pmat, qt, dec = _s5_params(s5_A_re[0], s5_A_im[0], s5_log_dt[0], s5_B_re[0], s5_B_im[0],
                                     s5_C_re[0], s5_C_im[0])
    n_grp = d // S5_GROUP_CH
    cols_hg = lambda w: w.reshape(w.shape[0], n_grp, S5_GROUP_CH).swapaxes(1, 2).reshape(w.shape[0], d)
    rows_hg = lambda w: cols_hg(w.T).T
    wi = w_in[0]
    wut = rows_hg(wi[:, :d].T).astype(bf16)
    wr = jnp.concatenate([wi[:, 3 * d:5 * d], cols_hg(wi[:, d:2 * d]), wi[:, 2 * d:3 * d],
                          jnp.pad(wi[:, 5 * d:], ((0, 0), (0, pad_h)))],
                         axis=1).astype(bf16)
    pad1 = lambda v: jnp.pad(v, (0, pad_h))[None, :]
    cw8 = jnp.broadcast_to(conv_w[0][:, None, :], (CONV_WIDTH, 8, conv_w.shape[2]))
    cb8 = jnp.broadcast_to(conv_b[0][None, :], (8, conv_b.shape[1]))
    u3, zs, zd, xc, dt = _in_proj(x2, norm_w[0][None, :], wut, wr, cw8, cb8, pad1(dt_bias[0]), _block(seq, 512), seq)
    drow = jnp.repeat(s5_D[0].reshape(-1, S5_GROUP_CH), S5_CHUNK, axis=1)[:, None, :]
    y3 = _s5_mix(u3, toep, pmat, qt, dec, drow, rows_per_seq)
    head_of = jnp.arange(d) // SSD_HEAD_DIM
    e2 = (jnp.arange(LANES)[:, None] == head_of[None, :]).astype(bf16)
    e2 = jnp.concatenate([e2, e2], axis=0)
    tile_of = jnp.arange(n_heads * SSD_CHUNK) // SSD_CHUNK
    sel = (jnp.arange(LANES)[:, None] == tile_of[None, :]).astype(bf16)
    sel3 = jnp.concatenate([sel, sel, sel], axis=0)
    yd = _ssd(xc, dt, zd, pad1(A_log[0]), jnp.repeat(ssd_D[0], SSD_HEAD_DIM)[None, :], ssd_norm_w[0][None, :], e2,
              sel3, batch, _block(seq, 512))
    wo = jnp.concatenate([rows_hg(w_out[0][:d]), w_out[0][d:]], axis=0).astype(bf16)
    out = _out_proj(x2, y3, zs, yd, p[0].reshape(m, -1), rows_hg(cols_hg(s5_w_glu[0])).astype(bf16),
                    cols_hg(s5_b_glu[0][None, :]), wo, ple_norm_w[0][None, :], w_ple_gate[0].astype(bf16),
                    w_ple_proj[0].astype(bf16), final_norm_w[None, :], _block(m, 512))
    return out.reshape(batch, seq, d)
```

```python
import functools

import jax
import jax.numpy as jnp
from jax import lax
from jax.experimental import pallas as pl
from jax.experimental.pallas import tpu as pltpu

f32 = jnp.float32
bf16 = jnp.bfloat16

EPS = 1e-6
LANES = 128
S5_GROUP_CH = 16
S5_STATE = 64
S5_CHUNK = 64
SSD_HEAD_DIM = 64
SSD_GROUPS = 4
SSD_STATE = 128
SSD_CHUNK = 128
CONV_WIDTH = 4
CONV_ROWS, CONV_COLS = 32, 128
VMEM_LIMIT = 56 * 1024 * 1024


def _cparams(*sem):
    return pltpu.CompilerParams(dimension_semantics=sem, vmem_limit_bytes=VMEM_LIMIT)


def _const_spec(shape):
    nd = len(shape)
    return pl.BlockSpec(shape, lambda *_: (0,) * nd, pipeline_mode=pl.Buffered(1))


def _swap_halves(x):
    return pltpu.roll(x, LANES // 2, x.ndim - 1)


def _s5_params_kernel(are_ref, aim_ref, ldt_ref, bt_ref, ca_ref, cr_ref, ci_ref,
                      toep_ref, p_ref, qt_ref, dec_ref, kv_ref):
    T, H = S5_CHUNK, S5_GROUP_CH
    ar2, ai2 = are_ref[0], aim_ref[0]
    dt = jnp.exp(ldt_ref[0])
    xr, xi = ar2 * dt, ai2 * dt
    lane1 = lax.broadcasted_iota(jnp.int32, (1, LANES), 1)
    sgn = jnp.where(lane1 < T, -1.0, 1.0).astype(f32)

    def cmul(dr, di, x):
        return dr * x + di * (sgn * _swap_halves(x))

    ea = jnp.exp(xr)
    abr, abi = ea * jnp.cos(xi), ea * jnp.sin(xi)
    den = ar2 * ar2 + ai2 * ai2
    nr, ni = abr - 1.0, abi
    qr = (nr * ar2 + ni * ai2) / den
    qi = (ni * ar2 - nr * ai2) / den
    bb = cmul(qr, qi, bt_ref[0])

    def powtab(t):
        e = jnp.exp(t * xr)
        return e * jnp.cos(t * xi), e * jnp.sin(t * xi)

    def zmul(ar_, ai_, br_, bi_):
        return ar_ * br_ - ai_ * bi_, ar_ * bi_ + ai_ * br_

    j8 = lax.broadcasted_iota(jnp.int32, (8, LANES), 0).astype(f32)
    s8r, s8i = powtab(jnp.full((1, LANES), 8.0, f32))
    up, down = [powtab(j8)], [powtab(7.0 - j8)]
    for _ in range(T // 8 - 1):
        up.append(zmul(*up[-1], s8r, s8i))
        down.append(zmul(*down[-1], s8r, s8i))
    pr0, pi0 = (jnp.concatenate([t[k] for t in up], axis=0) for k in range(2))
    pr_rev, pi_rev = (jnp.concatenate([t[k] for t in down[::-1]], axis=0) for k in range(2))
    pr1, pi1 = zmul(pr0, pi0, abr, abi)
    conj = -sgn
    for h in range(H):
        p_ref[0, h * T:(h + 1) * T, :] = cmul(pr_rev, pi_rev, bb[h:h + 1, :]).astype(p_ref.dtype)
        qt_ref[0, h * T:(h + 1) * T, :] = (conj * cmul(pr1, pi1, ca_ref[0, h:h + 1, :])).astype(qt_ref.dtype)

    pr_t, pi_t = s8r, s8i
    for _ in range((T // 8).bit_length() - 1):
        pr_t, pi_t = zmul(pr_t, pi_t, pr_t, pi_t)
    dec_ref[0] = jnp.zeros((8, LANES), f32)
    dec_ref[0, 0:1, :] = pr_t
    dec_ref[0, 1:2, :] = sgn * pi_t

    pows = jnp.where(lax.broadcasted_iota(jnp.int32, (T, LANES), 1) < T, pr0, -pi0)
    zeros = jnp.zeros((T, LANES), f32)
    rhs = jnp.concatenate([jnp.concatenate([pows, zeros], axis=1),
                           jnp.concatenate([zeros, pows], axis=1)], axis=0)
    cb = [cr_ref[0, hp:hp + 1, :] * bb + ci_ref[0, hp:hp + 1, :] * (sgn * _swap_halves(bb)) for hp in range(H)]
    lhs = jnp.concatenate([jnp.concatenate([cb[2 * m], cb[2 * m + 1]], axis=1) for m in range(H // 2)], axis=0)
    kv_ref[...] = lax.dot_general(lhs, rhs, (((1,), (1,)), ((), ())), precision=lax.Precision.HIGHEST,
                                  preferred_element_type=f32)

    row = lax.broadcasted_iota(jnp.int32, (T, LANES), 0)
    col = lax.broadcasted_iota(jnp.int32, (T, LANES), 1)
    causal = (col % T) >= row
    for h in range(H):
        for m in range(H // 2):
            v = jnp.broadcast_to(kv_ref[m * H + h:m * H + h + 1, :], (T, LANES))
            tile = pltpu.roll(v, 0, 1, stride=1, stride_axis=0)
            toep_ref[0, h * T:(h + 1) * T, m * LANES:(m + 1) * LANES] = jnp.where(causal, tile, 0.0).astype(toep_ref.dtype)


def _s5_params(s5_A_re, s5_A_im, s5_log_dt, s5_B_re, s5_B_im, s5_C_re, s5_C_im):
    G, N = s5_A_re.shape
    H, T = S5_GROUP_CH, S5_CHUNK
    dup = lambda a: jnp.concatenate([a, a], axis=-1)
    are = dup(s5_A_re)[:, None, :]
    aim = dup(s5_A_im)[:, None, :]
    ldt = jnp.broadcast_to(s5_log_dt[:, None, None], (G, 1, 2 * N))
    bt = jnp.concatenate([jnp.swapaxes(s5_B_re, 1, 2), jnp.swapaxes(s5_B_im, 1, 2)], axis=-1)
    ca = jnp.concatenate([s5_C_re, s5_C_im], axis=-1)
    cr, ci = dup(s5_C_re), dup(s5_C_im)
    g3 = lambda a: pl.BlockSpec((1,) + a.shape[1:], lambda g: (g, 0, 0))
    ins = (are, aim, ldt, bt, ca, cr, ci)
    return pl.pallas_call(
        _s5_params_kernel,
        out_shape=(jax.ShapeDtypeStruct((G, H * T, H * T), bf16),
                   jax.ShapeDtypeStruct((G, H * T, 2 * N), bf16),
                   jax.ShapeDtypeStruct((G, H * T, 2 * N), bf16),
                   jax.ShapeDtypeStruct((G, 8, 2 * N), f32)),
        grid=(G,),
        in_specs=[g3(a) for a in ins],
        out_specs=(pl.BlockSpec((1, H * T, H * T), lambda g: (g, 0, 0)),
                   pl.BlockSpec((1, H * T, 2 * N), lambda g: (g, 0, 0)),
                   pl.BlockSpec((1, H * T, 2 * N), lambda g: (g, 0, 0)),
                   pl.BlockSpec((1, 8, 2 * N), lambda g: (g, 0, 0))),
        scratch_shapes=[pltpu.VMEM((H // 2 * H, LANES), f32)],
        compiler_params=_cparams("arbitrary"),
        name="s5_params",
    )(*ins)


def _rms(x, w):
    return x * lax.rsqrt(jnp.mean(x * x, axis=-1, keepdims=True) + EPS) * w


def _conv_silu(xpad_ref, cw_ref, cb_ref, xc_ref, c_lo, c_hi):
    tb = xc_ref.shape[0]
    for c0 in range(c_lo, c_hi, CONV_COLS):
        cols = slice(c0, c0 + CONV_COLS)
        sub = lax.broadcasted_iota(jnp.int32, (8, CONV_COLS), 0)
        prev_rolled = [pltpu.roll(xpad_ref[0:8, cols], sh, 0) for sh in range(1, CONV_WIDTH)]
        for r0 in range(0, tb, CONV_ROWS):
            outs = []
            for r in range(r0, r0 + CONV_ROWS, 8):
                cur = xpad_ref[8 + r:16 + r, cols]
                acc = cb_ref[:, cols] + cw_ref[CONV_WIDTH - 1, :, cols] * cur
                cur_rolled = [pltpu.roll(cur, sh, 0) for sh in range(1, CONV_WIDTH)]
                for sh in range(1, CONV_WIDTH):
                    acc = acc + cw_ref[CONV_WIDTH - 1 - sh, :, cols] * jnp.where(sub < sh, prev_rolled[sh - 1],
                                                                                  cur_rolled[sh - 1])
                outs.append(acc * jax.nn.sigmoid(acc))
                prev_rolled = cur_rolled
            xc_ref[r0:r0 + CONV_ROWS, cols] = jnp.concatenate(outs, axis=0).astype(xc_ref.dtype)


def _in_proj_kernel(x_ref, nw_ref, wut_ref, wr_ref, cw_ref, cb_ref, dtb_ref,
                    u3_ref, zs_ref, zd_ref, xc_ref, dt_ref, xpad_ref, hn_ref, *, blocks_per_seq):
    tb, d = x_ref.shape
    i = pl.program_id(0)

    @pl.when(i == 0)
    def _():
        xpad_ref[tb:tb + 8, :] = jnp.zeros((8, xpad_ref.shape[1]), f32)

    @pl.when(i >= 0)
    def _():
        hn_ref[...] = _rms(x_ref[...], nw_ref[...]).astype(bf16)
        tail = xpad_ref[tb:tb + 8, :]
        xpad_ref[0:8, :] = jnp.where(i % blocks_per_seq != 0, tail, 0.0)

    hn = hn_ref[...]

    def z_cols(lo, hi, ref):
        ref[...] = jnp.dot(hn, wr_ref[:, lo:hi], preferred_element_type=f32).astype(ref.dtype)

    def dt_cols():
        dt_ref[...] = jax.nn.softplus(jnp.dot(hn, wr_ref[:, 4 * d:], preferred_element_type=f32) + dtb_ref[...])

    def u3_rows(jj):
        v = lax.dot_general(wut_ref[...], hn[jj * 2 * LANES:(jj + 1) * 2 * LANES, :],
                            (((1,), (1,)), ((), ())), preferred_element_type=f32)
        u3_ref[2 * jj] = v[:, :LANES]
        u3_ref[2 * jj + 1] = v[:, LANES:]

    n_slabs = 4
    slab = 2 * d // n_slabs
    for k in range(n_slabs):
        xpad_ref[8:8 + tb, k * slab:(k + 1) * slab] = jnp.dot(hn, wr_ref[:, k * slab:(k + 1) * slab],
                                                              preferred_element_type=f32)
    for jj in range(tb // (2 * LANES)):
        u3_rows(jj)
    for k in range(n_slabs):
        _conv_silu(xpad_ref, cw_ref, cb_ref, xc_ref, k * slab, (k + 1) * slab)
    z_cols(2 * d, 3 * d, zs_ref)
    z_cols(3 * d, 4 * d, zd_ref)
    dt_cols()


def _in_proj(x2, norm_w, wut, wr, cw, cb, dtb, tb, seq):
    m, d = x2.shape
    nr = wr.shape[1]
    tok = lambda i: (i, 0)
    return pl.pallas_call(
        functools.partial(_in_proj_kernel, blocks_per_seq=seq // tb),
        out_shape=(jax.ShapeDtypeStruct((m // LANES, d, LANES), f32),
                   jax.ShapeDtypeStruct((m, d), bf16),
                   jax.ShapeDtypeStruct((m, d), bf16),
                   jax.ShapeDtypeStruct((m, 2 * d), bf16),
                   jax.ShapeDtypeStruct((m, LANES), f32)),
        grid=(m // tb,),
        in_specs=[pl.BlockSpec((tb, d), tok),
                  _const_spec((1, d)), _const_spec((d, d)), _const_spec((d, nr)),
                  _const_spec(cw.shape), _const_spec(cb.shape), _const_spec(dtb.shape)],
        out_specs=(pl.BlockSpec((tb // LANES, d, LANES), lambda i: (i, 0, 0)),
                   pl.BlockSpec((tb, d), tok), pl.BlockSpec((tb, d), tok),
                   pl.BlockSpec((tb, 2 * d), tok), pl.BlockSpec((tb, LANES), tok)),
        scratch_shapes=[pltpu.VMEM((tb + 8, 2 * d), f32), pltpu.VMEM((tb, d), bf16)],
        compiler_params=_cparams("arbitrary"),
        name="in_proj",
    )(x2, norm_w, wut, wr, cw, cb, dtb)


class _CopyGroup:
    def __init__(self, copies):
        self.copies = copies

    def start(self):
        for cp in self.copies:
            cp.start()

    def wait(self):
        for cp in self.copies:
            cp.wait()


def _s5_mix_kernel(u_hbm, toep_ref, p_ref, qt_ref, dec_ref, d_ref, o_hbm, ubuf, ybuf, sem_in, sem_out, *,
                   rows_per_seq):
    g = pl.program_id(0)
    n_g = pl.num_programs(0)
    slot = g % 2
    _, H, J, _ = ubuf.shape

    def in_copy(grp, s):
        return _CopyGroup([pltpu.make_async_copy(u_hbm.at[:, grp * H + h, :], ubuf.at[s, h], sem_in.at[s])
                           for h in range(H)])

    def out_copy(grp, s):
        return _CopyGroup([pltpu.make_async_copy(ybuf.at[s, h], o_hbm.at[:, grp * H + h, :], sem_out.at[s])
                           for h in range(H)])

    @pl.when(g == 0)
    def _():
        in_copy(0, 0).start()

    @pl.when(g + 1 < n_g)
    def _():
        in_copy(g + 1, 1 - slot).start()

    in_copy(g, slot).wait()

    @pl.when(g >= 2)
    def _():
        out_copy(g - 2, slot).wait()

    u_ref = ubuf.at[slot]
    o_ref = ybuf.at[slot]
    half = LANES // 2
    lo = lax.broadcasted_iota(jnp.int32, (J, LANES), 1) < half
    e0, e1 = [], []
    for m in range(H // 2):
        a, b = u_ref[2 * m], u_ref[2 * m + 1]
        e0.append(jnp.where(lo, a, _swap_halves(b)))
        e1.append(jnp.where(lo, _swap_halves(a), b))
    e = jnp.concatenate([jnp.concatenate(e0, axis=1), jnp.concatenate(e1, axis=1)], axis=0)
    eb = e.astype(bf16)
    y = jnp.dot(eb, toep_ref[0], preferred_element_type=f32)
    s = jnp.dot(eb, p_ref[0], preferred_element_type=f32)
    s0, s1 = s[:J], s[J:]
    ar, ai = dec_ref[0, 0:1, :], dec_ref[0, 1:2, :]

    def cm(cr, ci, x):
        return cr * x + ci * _swap_halves(x)

    def csq(cr, ci):
        return cr * cr - ci * ci, 2.0 * cr * ci

    v = cm(ar, ai, s0) + s1
    k_idx = lax.broadcasted_iota(jnp.int32, (J, LANES), 0) % rows_per_seq
    w = v
    dr, di = csq(ar, ai)
    step = 1
    while step < rows_per_seq:
        sh = jnp.where(k_idx >= step, pltpu.roll(w, step, 0), 0.0)
        w = w + cm(dr, di, sh)
        dr, di = csq(dr, di)
        step *= 2
    h0 = jnp.where(k_idx >= 1, pltpu.roll(w, 1, 0), 0.0)
    h1 = cm(ar, ai, h0) + s0
    hb = jnp.concatenate([h0, h1], axis=0).astype(bf16)
    y = y + lax.dot_general(hb, qt_ref[0], (((1,), (1,)), ((), ())), preferred_element_type=f32)
    y = jax.nn.gelu(y + e * d_ref[0])
    for m in range(H // 2):
        y0 = y[:J, m * LANES:(m + 1) * LANES]
        y1 = y[J:, m * LANES:(m + 1) * LANES]
        o_ref[2 * m] = jnp.where(lo, y0, _swap_halves(y1))
        o_ref[2 * m + 1] = jnp.where(lo, _swap_halves(y0), y1)

    out_copy(g, slot).start()

    @pl.when(g == n_g - 1)
    def _():
        @pl.when(n_g >= 2)
        def _():
            out_copy(g - 1, 1 - slot).wait()
        out_copy(g, slot).wait()


def _s5_mix(u3, toep, p, qt, dec, drow, rows_per_seq):
    J = u3.shape[0]
    G, w, _ = toep.shape
    H = S5_GROUP_CH
    n2 = p.shape[2]
    return pl.pallas_call(
        functools.partial(_s5_mix_kernel, rows_per_seq=rows_per_seq),
        out_shape=jax.ShapeDtypeStruct(u3.shape, f32),
        grid=(G,),
        in_specs=[pl.BlockSpec(memory_space=pl.ANY),
                  pl.BlockSpec((1, w, w), lambda g: (g, 0, 0)),
                  pl.BlockSpec((1, w, n2), lambda g: (g, 0, 0)),
                  pl.BlockSpec((1, w, n2), lambda g: (g, 0, 0)),
                  pl.BlockSpec((1, 8, n2), lambda g: (g, 0, 0)),
                  pl.BlockSpec((1, 1, w), lambda g: (g, 0, 0))],
        out_specs=pl.BlockSpec(memory_space=pl.ANY),
        scratch_shapes=[pltpu.VMEM((2, H, J, LANES), f32), pltpu.VMEM((2, H, J, LANES), f32),
                        pltpu.SemaphoreType.DMA((2,)), pltpu.SemaphoreType.DMA((2,))],
        compiler_params=_cparams("arbitrary"),
        name="s5_mix",
    )(u3, toep, p, qt, dec, drow)


def _split_bf16(v, terms):
    parts, r = [], v
    for _ in range(terms):
        part = r.astype(bf16)
        parts.append(part)
        r = r - part.astype(f32)
    return jnp.concatenate(parts, axis=1)


def _ssd_kernel(xc_ref, dt_ref, zd_ref, alog_ref, dexp_ref, nw_ref, e2_ref, sel_ref, o_ref, state_ref):
    ts = xc_ref.shape[0]
    d = o_ref.shape[1]
    T, N, P = SSD_CHUNK, SSD_STATE, SSD_HEAD_DIM
    gw = d // SSD_GROUPS
    hpg = gw // P
    n_chunks = ts // T

    @pl.when(pl.program_id(1) == 0)
    def _():
        state_ref[...] = jnp.zeros_like(state_ref)

    dt = dt_ref[...]
    a = dt * (-jnp.exp(alog_ref[...]))
    li = lax.broadcasted_iota(jnp.int32, (T, T), 0)
    si = lax.broadcasted_iota(jnp.int32, (T, T), 1)
    causal = li >= si
    tri = causal.astype(f32)
    a_cs_c = [jnp.dot(tri, a[c * T:(c + 1) * T], precision=lax.Precision.HIGHEST, preferred_element_type=f32)
              for c in range(n_chunks)]
    a_cs = jnp.concatenate(a_cs_c, axis=0)
    a_last = jnp.concatenate([jnp.broadcast_to(v[T - 1:T, :], (T, LANES)) for v in a_cs_c], axis=0)
    w = jnp.exp(a_last - a_cs) * dt
    e = jnp.exp(a_cs)
    cd = jnp.concatenate([jnp.broadcast_to(jnp.exp(v[T - 1:T, :]), (8, LANES)) for v in a_cs_c], axis=0)
    ex = jnp.dot(_split_bf16(jnp.concatenate([w, e, cd], axis=0), 2), e2_ref[...], preferred_element_type=f32)
    w_x, e_x, cd_x = ex[:ts], ex[ts:2 * ts], ex[2 * ts:]
    a_parts = _split_bf16(a_cs, 2)
    a_cs_r = a_parts[:, :LANES].astype(f32) + a_parts[:, LANES:].astype(f32)
    colb = jnp.dot(a_parts, sel_ref[...], preferred_element_type=f32)
    head_of_lane = lax.broadcasted_iota(jnp.int32, (T, gw), 1) // P
    for c in range(n_chunks):
        sl = slice(c * T, (c + 1) * T)
        a_cst = a_cs_r[sl].T
        dtt = dt[sl].T
        ys = []
        for q in range(SSD_GROUPS):
            cs = slice(q * gw, (q + 1) * gw)
            bq = xc_ref[sl, d + q * N:d + (q + 1) * N]
            cq = xc_ref[sl, d + SSD_GROUPS * N + q * N:d + SSD_GROUPS * N + (q + 1) * N]
            xq = xc_ref[sl, cs]
            g = lax.dot_general(cq, bq, (((1,), (1,)), ((), ())), preferred_element_type=f32)
            scores, xbd = [], []
            for hh in range(hpg):
                h = q * hpg + hh
                diff = colb[sl, h * T:(h + 1) * T] - a_cst[h:h + 1, :]
                lm = jnp.exp(jnp.where(causal, diff, -1e30))
                scores.append((g * lm * dtt[h:h + 1, :]).astype(bf16))
                xbd.append(jnp.where(head_of_lane == hh, xq, jnp.zeros_like(xq)))
            y_diag = jnp.dot(jnp.concatenate(scores, axis=1), jnp.concatenate(xbd, axis=0),
                             preferred_element_type=f32)
            st = state_ref[:, cs]
            y_off = jnp.dot(cq, st.astype(bf16), preferred_element_type=f32) * e_x[sl, cs]
            xw = xq * w_x[sl, cs].astype(bf16)
            s_new = lax.dot_general(bq, xw, (((0,), (0,)), ((), ())), preferred_element_type=f32)
            state_ref[:, cs] = st * cd_x[8 * c:8 * c + 1, cs] + s_new
            ys.append(y_diag + y_off)
        y = jnp.concatenate(ys, axis=1) + dexp_ref[...] * xc_ref[sl, :d].astype(f32)
        z = zd_ref[sl, :].astype(f32)
        gt = y * (z * jax.nn.sigmoid(z))
        gn = []
        for q in range(SSD_GROUPS):
            gq = gt[:, q * gw:(q + 1) * gw]
            gn.append(gq * lax.rsqrt(jnp.sum(gq * gq, axis=-1, keepdims=True) * (1.0 / gw) + EPS))
        o_ref[sl, :] = (jnp.concatenate(gn, axis=1) * nw_ref[...]).astype(o_ref.dtype)


def _ssd(xc, dt, zd, alog, dexp, nw, e2, sel, batch, ts):
    m, cdim = xc.shape
    d = zd.shape[1]
    nb = m // batch // ts
    tok = lambda b, i: (b * nb + i, 0)
    return pl.pallas_call(
        _ssd_kernel,
        out_shape=jax.ShapeDtypeStruct((m, d), bf16),
        grid=(batch, nb),
        in_specs=[pl.BlockSpec((ts, cdim), tok), pl.BlockSpec((ts, LANES), tok), pl.BlockSpec((ts, d), tok),
                  _const_spec(alog.shape), _const_spec(dexp.shape), _const_spec(nw.shape), _const_spec(e2.shape),
                  _const_spec(sel.shape)],
        out_specs=pl.BlockSpec((ts, d), tok),
        scratch_shapes=[pltpu.VMEM((SSD_STATE, d), f32)],
        compiler_params=_cparams("arbitrary", "arbitrary"),
        name="ssd",
    )(xc, dt, zd, alog, dexp, nw, e2, sel)


def _out_kernel(x_ref, y3_ref, zs_ref, yd_ref, p_ref, wglu_ref, bglu_ref, wo_ref, pnw_ref, wg_ref, wp_ref, fnw_ref,
                o_ref):
    d = x_ref.shape[1]
    y = jnp.concatenate([y3_ref[jj].T for jj in range(y3_ref.shape[0])], axis=0)
    glu = y * jax.nn.sigmoid(jnp.dot(y.astype(bf16), wglu_ref[...], preferred_element_type=f32) + bglu_ref[...])
    z = zs_ref[...].astype(f32)
    ys5 = glu * (z * jax.nn.sigmoid(z))
    h = x_ref[...] + jnp.dot(ys5.astype(bf16), wo_ref[:d, :], preferred_element_type=f32)
    h = h + jnp.dot(yd_ref[...], wo_ref[d:, :], preferred_element_type=f32)
    gate = jax.nn.sigmoid(jnp.dot(_rms(h, pnw_ref[...]).astype(bf16), wg_ref[...], preferred_element_type=f32))
    h = h + jnp.dot(p_ref[...].astype(bf16), wp_ref[...], preferred_element_type=f32) * gate
    o_ref[...] = _rms(h, fnw_ref[...])


def _out_proj(x2, y3, zs, yd, p2, wglu, bglu, wo, pnw, wg, wp, fnw, tb):
    m, d = x2.shape
    tok = lambda i: (i, 0)
    return pl.pallas_call(
        _out_kernel,
        out_shape=jax.ShapeDtypeStruct((m, d), f32),
        grid=(m // tb,),
        in_specs=[pl.BlockSpec((tb, d), tok),
                  pl.BlockSpec((tb // LANES, d, LANES), lambda i: (i, 0, 0)),
                  pl.BlockSpec((tb, d), tok), pl.BlockSpec((tb, d), tok), pl.BlockSpec((tb, p2.shape[1]), tok),
                  _const_spec(wglu.shape), _const_spec(bglu.shape), _const_spec(wo.shape), _const_spec(pnw.shape),
                  _const_spec(wg.shape), _const_spec(wp.shape), _const_spec(fnw.shape)],
        out_specs=pl.BlockSpec((tb, d), tok),
        compiler_params=_cparams("parallel"),
        name="out_proj",
    )(x2, y3, zs, yd, p2, wglu, bglu, wo, pnw, wg, wp, fnw)


def _block(m, want):
    b = want
    while m % b:
        b //= 2
    return b


def kernel(x, p, norm_w, w_in, s5_A_re, s5_A_im, s5_log_dt, s5_B_re, s5_B_im, s5_C_re, s5_C_im, s5_D, s5_w_glu, s5_b_glu, conv_w, conv_b, dt_bias, A_log, ssd_D, ssd_norm_w, w_out, ple_norm_w, w_ple_gate, w_ple_proj, final_norm_w):
    batch, seq, d = x.shape
    assert norm_w.shape[0] == 1, "single-layer problem: the final norm is fused into the layer's last kernel"
    m = batch * seq
    n_heads = dt_bias.shape[1]
    assert seq % (2 * LANES) == 0 and d % LANES == 0
    rows_per_seq = seq // LANES
    assert rows_per_seq & (rows_per_seq - 1) == 0
    pad_h = LANES - n_heads
    x2 = x.reshape(m, d)
    toep, pmat, qt, dec = _s5_params(s5_A_re[0], s5_A_im[0], s5_log_dt[0], s5_B_re[0], s5_B_im[0],
                                     s5_C_re[0], s5_C_im[0])
    wi = w_in[0].astype(bf16)
    wut = wi[:, :d].T
    wr = jnp.concatenate([wi[:, 3 * d:5 * d], wi[:, d:3 * d], jnp.pad(wi[:, 5 * d:], ((0, 0), (0, pad_h)))],
                         axis=1)
    pad1 = lambda v: jnp.pad(v, (0, pad_h))[None, :]
    cw8 = jnp.broadcast_to(conv_w[0][:, None, :], (CONV_WIDTH, 8, conv_w.shape[2]))
    cb8 = jnp.broadcast_to(conv_b[0][None, :], (8, conv_b.shape[1]))
    u3, zs, zd, xc, dt = _in_proj(x2, norm_w[0][None, :], wut, wr, cw8, cb8, pad1(dt_bias[0]), _block(seq, 512), seq)
    drow = jnp.repeat(s5_D[0].reshape(-1, S5_GROUP_CH), S5_CHUNK, axis=1)[:, None, :]
    y3 = _s5_mix(u3, toep, pmat, qt, dec, drow, rows_per_seq)
    head_of = jnp.arange(d) // SSD_HEAD_DIM
    e2 = (jnp.arange(LANES)[:, None] == head_of[None, :]).astype(bf16)
    e2 = jnp.concatenate([e2, e2], axis=0)
    tile_of = jnp.arange(n_heads * SSD_CHUNK) // SSD_CHUNK
    sel = (jnp.arange(LANES)[:, None] == tile_of[None, :]).astype(bf16)
    sel = jnp.concatenate([sel, sel], axis=0)
    yd = _ssd(xc, dt, zd, pad1(A_log[0]), jnp.repeat(ssd_D[0], SSD_HEAD_DIM)[None, :], ssd_norm_w[0][None, :], e2,
              sel, batch, _block(seq, 512))
    out = _out_proj(x2, y3, zs, yd, p[0].reshape(m, -1), s5_w_glu[0].astype(bf16), s5_b_glu[0][None, :],
                    w_out[0].astype(bf16), ple_norm_w[0][None, :], w_ple_gate[0].astype(bf16),
                    w_ple_proj[0].astype(bf16), final_norm_w[None, :], _block(m, 512))
    return out.reshape(batch, seq, d)
```

```python
import functools

import jax
import jax.numpy as jnp
from jax import lax
from jax.experimental import pallas as pl
from jax.experimental.pallas import tpu as pltpu

f32 = jnp.float32
bf16 = jnp.bfloat16

EPS = 1e-6
LANES = 128
S5_GROUP_CH = 16
S5_STATE = 64
S5_CHUNK = 64
S5_GROUPS_PER_STEP = 2
SSD_HEAD_DIM = 64
SSD_GROUPS = 4
SSD_STATE = 128
SSD_CHUNK = 128
CONV_WIDTH = 4
CONV_ROWS, CONV_COLS = 32, 128
VMEM_LIMIT = 56 * 1024 * 1024


def _cparams(*sem):
    return pltpu.CompilerParams(dimension_semantics=sem, vmem_limit_bytes=VMEM_LIMIT)


def _const_spec(shape):
    nd = len(shape)
    return pl.BlockSpec(shape, lambda *_: (0,) * nd, pipeline_mode=pl.Buffered(1))


def _swap_halves(x):
    return pltpu.roll(x, LANES // 2, x.ndim - 1)


def _s5_params_kernel(are_ref, aim_ref, ldt_ref, bt_ref, ca_ref, cr_ref, ci_ref,
                      toep_ref, p_ref, qt_ref, dec_ref, kv_ref):
    T, H = S5_CHUNK, S5_GROUP_CH
    ar2, ai2 = are_ref[0], aim_ref[0]
    dt = jnp.exp(ldt_ref[0])
    xr, xi = ar2 * dt, ai2 * dt
    lane1 = lax.broadcasted_iota(jnp.int32, (1, LANES), 1)
    sgn = jnp.where(lane1 < T, -1.0, 1.0).astype(f32)

    def cmul(dr, di, x):
        return dr * x + di * (sgn * _swap_halves(x))

    ea = jnp.exp(xr)
    abr, abi = ea * jnp.cos(xi), ea * jnp.sin(xi)
    den = ar2 * ar2 + ai2 * ai2
    nr, ni = abr - 1.0, abi
    qr = (nr * ar2 + ni * ai2) / den
    qi = (ni * ar2 - nr * ai2) / den
    bb = cmul(qr, qi, bt_ref[0])

    def powtab(t):
        e = jnp.exp(t * xr)
        return e * jnp.cos(t * xi), e * jnp.sin(t * xi)

    def zmul(ar_, ai_, br_, bi_):
        return ar_ * br_ - ai_ * bi_, ar_ * bi_ + ai_ * br_

    j8 = lax.broadcasted_iota(jnp.int32, (8, LANES), 0).astype(f32)
    s8r, s8i = powtab(jnp.full((1, LANES), 8.0, f32))
    up, down = [powtab(j8)], [powtab(7.0 - j8)]
    for _ in range(T // 8 - 1):
        up.append(zmul(*up[-1], s8r, s8i))
        down.append(zmul(*down[-1], s8r, s8i))
    pr0, pi0 = (jnp.concatenate([t[k] for t in up], axis=0) for k in range(2))
    pr_rev, pi_rev = (jnp.concatenate([t[k] for t in down[::-1]], axis=0) for k in range(2))
    pr1, pi1 = zmul(pr0, pi0, abr, abi)
    conj = -sgn
    for h in range(H):
        p_ref[0, h * T:(h + 1) * T, :] = cmul(pr_rev, pi_rev, bb[h:h + 1, :]).astype(p_ref.dtype)
        qt_ref[0, h * T:(h + 1) * T, :] = (conj * cmul(pr1, pi1, ca_ref[0, h:h + 1, :])).astype(qt_ref.dtype)

    pr_t, pi_t = s8r, s8i
    for _ in range((T // 8).bit_length() - 1):
        pr_t, pi_t = zmul(pr_t, pi_t, pr_t, pi_t)
    dec_ref[0] = jnp.zeros((8, LANES), f32)
    dec_ref[0, 0:1, :] = pr_t
    dec_ref[0, 1:2, :] = sgn * pi_t

    pows = jnp.where(lax.broadcasted_iota(jnp.int32, (T, LANES), 1) < T, pr0, -pi0)
    zeros = jnp.zeros((T, LANES), f32)
    rhs = jnp.concatenate([jnp.concatenate([pows, zeros], axis=1),
                           jnp.concatenate([zeros, pows], axis=1)], axis=0)
    cb = [cr_ref[0, hp:hp + 1, :] * bb + ci_ref[0, hp:hp + 1, :] * (sgn * _swap_halves(bb)) for hp in range(H)]
    lhs = jnp.concatenate([jnp.concatenate([cb[2 * m], cb[2 * m + 1]], axis=1) for m in range(H // 2)], axis=0)
    kv_ref[...] = lax.dot_general(lhs, rhs, (((1,), (1,)), ((), ())), precision=lax.Precision.HIGHEST,
                                  preferred_element_type=f32)

    row = lax.broadcasted_iota(jnp.int32, (T, LANES), 0)
    col = lax.broadcasted_iota(jnp.int32, (T, LANES), 1)
    causal = (col % T) >= row
    for h in range(H):
        for m in range(H // 2):
            v = jnp.broadcast_to(kv_ref[m * H + h:m * H + h + 1, :], (T, LANES))
            tile = pltpu.roll(v, 0, 1, stride=1, stride_axis=0)
            toep_ref[0, h * T:(h + 1) * T, m * LANES:(m + 1) * LANES] = jnp.where(causal, tile, 0.0).astype(toep_ref.dtype)


def _s5_params(s5_A_re, s5_A_im, s5_log_dt, s5_B_re, s5_B_im, s5_C_re, s5_C_im):
    G, N = s5_A_re.shape
    H, T = S5_GROUP_CH, S5_CHUNK
    dup = lambda a: jnp.concatenate([a, a], axis=-1)
    are = dup(s5_A_re)[:, None, :]
    aim = dup(s5_A_im)[:, None, :]
    ldt = jnp.broadcast_to(s5_log_dt[:, None, None], (G, 1, 2 * N))
    bt = jnp.concatenate([jnp.swapaxes(s5_B_re, 1, 2), jnp.swapaxes(s5_B_im, 1, 2)], axis=-1)
    ca = jnp.concatenate([s5_C_re, s5_C_im], axis=-1)
    cr, ci = dup(s5_C_re), dup(s5_C_im)
    g3 = lambda a: pl.BlockSpec((1,) + a.shape[1:], lambda g: (g, 0, 0))
    ins = (are, aim, ldt, bt, ca, cr, ci)
    return pl.pallas_call(
        _s5_params_kernel,
        out_shape=(jax.ShapeDtypeStruct((G, H * T, H * T), bf16),
                   jax.ShapeDtypeStruct((G, H * T, 2 * N), bf16),
                   jax.ShapeDtypeStruct((G, H * T, 2 * N), bf16),
                   jax.ShapeDtypeStruct((G, 8, 2 * N), f32)),
        grid=(G,),
        in_specs=[g3(a) for a in ins],
        out_specs=(pl.BlockSpec((1, H * T, H * T), lambda g: (g, 0, 0)),
                   pl.BlockSpec((1, H * T, 2 * N), lambda g: (g, 0, 0)),
                   pl.BlockSpec((1, H * T, 2 * N), lambda g: (g, 0, 0)),
                   pl.BlockSpec((1, 8, 2 * N), lambda g: (g, 0, 0))),
        scratch_shapes=[pltpu.VMEM((H // 2 * H, LANES), f32)],
        compiler_params=_cparams("arbitrary"),
        name="s5_params",
    )(*ins)


def _rms(x, w):
    return x * lax.rsqrt(jnp.mean(x * x, axis=-1, keepdims=True) + EPS) * w


def _conv_silu(xpad_ref, cw_ref, cb_ref, xc_ref, c_lo, c_hi):
    tb = xc_ref.shape[0]
    for c0 in range(c_lo, c_hi, CONV_COLS):
        cols = slice(c0, c0 + CONV_COLS)
        sub = lax.broadcasted_iota(jnp.int32, (8, CONV_COLS), 0)
        prev_rolled = [pltpu.roll(xpad_ref[0:8, cols], sh, 0) for sh in range(1, CONV_WIDTH)]
        for r0 in range(0, tb, CONV_ROWS):
            outs = []
            for r in range(r0, r0 + CONV_ROWS, 8):
                cur = xpad_ref[8 + r:16 + r, cols]
                acc = cb_ref[:, cols] + cw_ref[CONV_WIDTH - 1, :, cols] * cur
                cur_rolled = [pltpu.roll(cur, sh, 0) for sh in range(1, CONV_WIDTH)]
                for sh in range(1, CONV_WIDTH):
                    acc = acc + cw_ref[CONV_WIDTH - 1 - sh, :, cols] * jnp.where(sub < sh, prev_rolled[sh - 1],
                                                                                  cur_rolled[sh - 1])
                outs.append(acc * jax.nn.sigmoid(acc))
                prev_rolled = cur_rolled
            xc_ref[r0:r0 + CONV_ROWS, cols] = jnp.concatenate(outs, axis=0).astype(xc_ref.dtype)


def _in_proj_kernel(x_ref, nw_ref, wut_ref, wr_ref, cw_ref, cb_ref, dtb_ref,
                    u3_ref, zs_ref, zd_ref, xc_ref, dt_ref, xpad_ref, hn_ref, *, blocks_per_seq):
    tb, d = x_ref.shape
    i = pl.program_id(0)

    @pl.when(i == 0)
    def _():
        xpad_ref[tb:tb + 8, :] = jnp.zeros((8, xpad_ref.shape[1]), f32)

    @pl.when(i >= 0)
    def _():
        hn_ref[...] = _rms(x_ref[...], nw_ref[...]).astype(bf16)
        tail = xpad_ref[tb:tb + 8, :]
        xpad_ref[0:8, :] = jnp.where(i % blocks_per_seq != 0, tail, 0.0)

    hn = hn_ref[...]

    def z_cols(lo, hi, ref):
        ref[...] = jnp.dot(hn, wr_ref[:, lo:hi], preferred_element_type=f32).astype(ref.dtype)

    def u3_rows(jj):
        v = lax.dot_general(wut_ref[...], hn[jj * 2 * LANES:(jj + 1) * 2 * LANES, :],
                            (((1,), (1,)), ((), ())), preferred_element_type=f32)
        u3_ref[2 * jj] = v[:d, :LANES]
        u3_ref[2 * jj + 1] = v[:d, LANES:]
        dt_ref[:, jj * 2 * LANES:(jj + 1) * 2 * LANES] = jax.nn.softplus(v[d:, :] + dtb_ref[...])

    n_slabs = 4
    slab = 2 * d // n_slabs
    for k in range(n_slabs):
        xpad_ref[8:8 + tb, k * slab:(k + 1) * slab] = jnp.dot(hn, wr_ref[:, k * slab:(k + 1) * slab],
                                                              preferred_element_type=f32)
    for jj in range(tb // (2 * LANES)):
        u3_rows(jj)
    for k in range(n_slabs):
        _conv_silu(xpad_ref, cw_ref, cb_ref, xc_ref, k * slab, (k + 1) * slab)
    z_cols(2 * d, 3 * d, zs_ref)
    z_cols(3 * d, 4 * d, zd_ref)


def _in_proj(x2, norm_w, wut, wr, cw, cb, dtb, tb, seq):
    m, d = x2.shape
    nr = wr.shape[1]
    tok = lambda i: (i, 0)
    return pl.pallas_call(
        functools.partial(_in_proj_kernel, blocks_per_seq=seq // tb),
        out_shape=(jax.ShapeDtypeStruct((m // LANES, d, LANES), f32),
                   jax.ShapeDtypeStruct((m, d), bf16),
                   jax.ShapeDtypeStruct((m, d), bf16),
                   jax.ShapeDtypeStruct((m, 2 * d), bf16),
                   jax.ShapeDtypeStruct((dtb.shape[0], m), f32)),
        grid=(m // tb,),
        in_specs=[pl.BlockSpec((tb, d), tok),
                  _const_spec((1, d)), _const_spec(wut.shape), _const_spec((d, nr)),
                  _const_spec(cw.shape), _const_spec(cb.shape), _const_spec(dtb.shape)],
        out_specs=(pl.BlockSpec((tb // LANES, d, LANES), lambda i: (i, 0, 0)),
                   pl.BlockSpec((tb, d), tok), pl.BlockSpec((tb, d), tok),
                   pl.BlockSpec((tb, 2 * d), tok), pl.BlockSpec((dtb.shape[0], tb), lambda i: (0, i))),
        scratch_shapes=[pltpu.VMEM((tb + 8, 2 * d), f32), pltpu.VMEM((tb, d), bf16)],
        compiler_params=_cparams("arbitrary"),
        name="in_proj",
    )(x2, norm_w, wut, wr, cw, cb, dtb)


class _CopyGroup:
    def __init__(self, copies):
        self.copies = copies

    def start(self):
        for cp in self.copies:
            cp.start()

    def wait(self):
        for cp in self.copies:
            cp.wait()


def _s5_mix_kernel(u_hbm, toep_ref, p_ref, qt_ref, dec_ref, d_ref, o_hbm, ubuf, ybuf, sem_in, sem_out, *,
                   rows_per_seq):
    g = pl.program_id(0)
    n_g = pl.num_programs(0)
    slot = g % 2
    HC = ubuf.shape[1]

    def in_copy(step, s):
        return _CopyGroup([pltpu.make_async_copy(u_hbm.at[:, step * HC + h, :], ubuf.at[s, h], sem_in.at[s])
                           for h in range(HC)])

    def out_copy(step, s):
        return _CopyGroup([pltpu.make_async_copy(ybuf.at[s, h], o_hbm.at[:, step * HC + h, :], sem_out.at[s])
                           for h in range(HC)])

    @pl.when(g == 0)
    def _():
        in_copy(0, 0).start()

    @pl.when(g + 1 < n_g)
    def _():
        in_copy(g + 1, 1 - slot).start()

    in_copy(g, slot).wait()

    @pl.when(g >= 2)
    def _():
        out_copy(g - 2, slot).wait()

    for sg in range(HC // S5_GROUP_CH):
        _s5_group(ubuf.at[slot], ybuf.at[slot], sg * S5_GROUP_CH, toep_ref[sg], p_ref[sg], qt_ref[sg], dec_ref[sg],
                  d_ref[sg], rows_per_seq)

    out_copy(g, slot).start()

    @pl.when(g == n_g - 1)
    def _():
        @pl.when(n_g >= 2)
        def _():
            out_copy(g - 1, 1 - slot).wait()
        out_copy(g, slot).wait()


def _s5_group(u_ref, o_ref, c0, toep, pmat, qt, dec, drow, rows_per_seq):
    J = u_ref.shape[1]
    H = S5_GROUP_CH
    half = LANES // 2
    lo = lax.broadcasted_iota(jnp.int32, (J, LANES), 1) < half
    e0, e1 = [], []
    for m in range(H // 2):
        a, b = u_ref[c0 + 2 * m], u_ref[c0 + 2 * m + 1]
        e0.append(jnp.where(lo, a, _swap_halves(b)))
        e1.append(jnp.where(lo, _swap_halves(a), b))
    e = jnp.concatenate([jnp.concatenate(e0, axis=1), jnp.concatenate(e1, axis=1)], axis=0)
    eb = e.astype(bf16)
    y = jnp.dot(eb, toep, preferred_element_type=f32)
    s = jnp.dot(eb, pmat, preferred_element_type=f32)
    s0, s1 = s[:J], s[J:]
    ar, ai = dec[0:1, :], dec[1:2, :]

    def cm(cr, ci, x):
        return cr * x + ci * _swap_halves(x)

    def csq(cr, ci):
        return cr * cr - ci * ci, 2.0 * cr * ci

    v = cm(ar, ai, s0) + s1
    k_idx = lax.broadcasted_iota(jnp.int32, (J, LANES), 0) % rows_per_seq
    w = v
    dr, di = csq(ar, ai)
    step = 1
    while step < rows_per_seq:
        sh = jnp.where(k_idx >= step, pltpu.roll(w, step, 0), 0.0)
        w = w + cm(dr, di, sh)
        dr, di = csq(dr, di)
        step *= 2
    h0 = jnp.where(k_idx >= 1, pltpu.roll(w, 1, 0), 0.0)
    h1 = cm(ar, ai, h0) + s0
    hb = jnp.concatenate([h0, h1], axis=0).astype(bf16)
    y = y + lax.dot_general(hb, qt, (((1,), (1,)), ((), ())), preferred_element_type=f32)
    y = jax.nn.gelu(y + e * drow)
    for m in range(H // 2):
        y0 = y[:J, m * LANES:(m + 1) * LANES]
        y1 = y[J:, m * LANES:(m + 1) * LANES]
        o_ref[c0 + 2 * m] = jnp.where(lo, y0, _swap_halves(y1))
        o_ref[c0 + 2 * m + 1] = jnp.where(lo, _swap_halves(y0), y1)


def _s5_mix(u3, toep, p, qt, dec, drow, rows_per_seq):
    J = u3.shape[0]
    G, w, _ = toep.shape
    gps = S5_GROUPS_PER_STEP
    hc = gps * S5_GROUP_CH
    n2 = p.shape[2]
    per_step = lambda a: pl.BlockSpec((gps,) + a.shape[1:], lambda g: (g, 0, 0))
    return pl.pallas_call(
        functools.partial(_s5_mix_kernel, rows_per_seq=rows_per_seq),
        out_shape=jax.ShapeDtypeStruct(u3.shape, f32),
        grid=(G // gps,),
        in_specs=[pl.BlockSpec(memory_space=pl.ANY),
                  per_step(toep), per_step(p), per_step(qt), per_step(dec), per_step(drow)],
        out_specs=pl.BlockSpec(memory_space=pl.ANY),
        scratch_shapes=[pltpu.VMEM((2, hc, J, LANES), f32), pltpu.VMEM((2, hc, J, LANES), f32),
                        pltpu.SemaphoreType.DMA((2,)), pltpu.SemaphoreType.DMA((2,))],
        compiler_params=_cparams("arbitrary"),
        name="s5_mix",
    )(u3, toep, p, qt, dec, drow)


def _split_bf16_rows(v):
    hi = v.astype(bf16)
    lo = (v - hi.astype(f32)).astype(bf16)
    return jnp.concatenate([hi, lo], axis=0)


def _ssd_kernel(xc_ref, dt_ref, zd_ref, alog_ref, dexp_ref, nw_ref, e2_ref, sel_ref, o_ref, state_ref):
    ts = xc_ref.shape[0]
    d = o_ref.shape[1]
    T, N, P = SSD_CHUNK, SSD_STATE, SSD_HEAD_DIM
    gw = d // SSD_GROUPS
    hpg = gw // P
    n_chunks = ts // T

    @pl.when(pl.program_id(1) == 0)
    def _():
        state_ref[...] = jnp.zeros_like(state_ref)

    dtt = dt_ref[...]
    neg_a = -jnp.exp(alog_ref[...])
    li = lax.broadcasted_iota(jnp.int32, (T, T), 0)
    si = lax.broadcasted_iota(jnp.int32, (T, T), 1)
    causal = li >= si
    triu = (li <= si).astype(f32)
    acs_c, w_c = [], []
    for c in range(n_chunks):
        dtc = dtt[:, c * T:(c + 1) * T]
        acs = jnp.dot(dtc * neg_a, triu, precision=lax.Precision.HIGHEST, preferred_element_type=f32)
        acs_c.append(acs)
        w_c.append(jnp.exp(acs[:, T - 1:T] - acs) * dtc)
    acs_t = jnp.concatenate(acs_c, axis=1)
    we_parts = _split_bf16_rows(jnp.concatenate(w_c + [jnp.exp(acs_t)], axis=1))
    ex = lax.dot_general(we_parts, e2_ref[...], (((0,), (0,)), ((), ())), preferred_element_type=f32)
    w_x, e_x = ex[:ts], ex[ts:]
    a_parts = _split_bf16_rows(acs_t)
    n_h = acs_t.shape[0]
    acs_r = a_parts[:n_h].astype(f32) + a_parts[n_h:].astype(f32)
    colb = lax.dot_general(a_parts, sel_ref[...], (((0,), (0,)), ((), ())),
                           preferred_element_type=f32)
    head_of_lane = lax.broadcasted_iota(jnp.int32, (T, gw), 1) // P
    for c in range(n_chunks):
        sl = slice(c * T, (c + 1) * T)
        a_cst = acs_r[:, sl]
        ys = []
        for q in range(SSD_GROUPS):
            cs = slice(q * gw, (q + 1) * gw)
            bq = xc_ref[sl, d + q * N:d + (q + 1) * N]
            cq = xc_ref[sl, d + SSD_GROUPS * N + q * N:d + SSD_GROUPS * N + (q + 1) * N]
            xq = xc_ref[sl, cs]
            g = lax.dot_general(cq, bq, (((1,), (1,)), ((), ())), preferred_element_type=f32)
            scores, xbd = [], []
            for hh in range(hpg):
                h = q * hpg + hh
                diff = colb[sl, h * T:(h + 1) * T] - a_cst[h:h + 1, :]
                lm = jnp.exp(jnp.where(causal, diff, -1e30))
                scores.append((g * lm * dtt[h:h + 1, sl]).astype(bf16))
                xbd.append(jnp.where(head_of_lane == hh, xq, jnp.zeros_like(xq)))
            y_diag = jnp.dot(jnp.concatenate(scores, axis=1), jnp.concatenate(xbd, axis=0),
                             preferred_element_type=f32)
            st = state_ref[:, cs]
            y_off = jnp.dot(cq, st.astype(bf16), preferred_element_type=f32) * e_x[sl, cs]
            xw = xq * w_x[sl, cs].astype(bf16)
            s_new = lax.dot_general(bq, xw, (((0,), (0,)), ((), ())), preferred_element_type=f32)
            state_ref[:, cs] = st * e_x[(c + 1) * T - 1:(c + 1) * T, cs] + s_new
            ys.append(y_diag + y_off)
        y = jnp.concatenate(ys, axis=1) + dexp_ref[...] * xc_ref[sl, :d].astype(f32)
        z = zd_ref[sl, :].astype(f32)
        gt = y * (z * jax.nn.sigmoid(z))
        gn = []
        for q in range(SSD_GROUPS):
            gq = gt[:, q * gw:(q + 1) * gw]
            gn.append(gq * lax.rsqrt(jnp.sum(gq * gq, axis=-1, keepdims=True) * (1.0 / gw) + EPS))
        o_ref[sl, :] = (jnp.concatenate(gn, axis=1) * nw_ref[...]).astype(o_ref.dtype)


def _ssd(xc, dt, zd, alog, dexp, nw, e2, sel, batch, ts):
    m, cdim = xc.shape
    d = zd.shape[1]
    nb = m // batch // ts
    tok = lambda b, i: (b * nb + i, 0)
    return pl.pallas_call(
        _ssd_kernel,
        out_shape=jax.ShapeDtypeStruct((m, d), bf16),
        grid=(batch, nb),
        in_specs=[pl.BlockSpec((ts, cdim), tok), pl.BlockSpec((dt.shape[0], ts), lambda b, i: (0, b * nb + i)),
                  pl.BlockSpec((ts, d), tok), _const_spec(alog.shape), _const_spec(dexp.shape), _const_spec(nw.shape), _const_spec(e2.shape),
                  _const_spec(sel.shape)],
        out_specs=pl.BlockSpec((ts, d), tok),
        scratch_shapes=[pltpu.VMEM((SSD_STATE, d), f32)],
        compiler_params=_cparams("arbitrary", "arbitrary"),
        name="ssd",
    )(xc, dt, zd, alog, dexp, nw, e2, sel)


def _out_kernel(x_ref, y3_ref, zs_ref, yd_ref, p_ref, wglu_ref, bglu_ref, wo_ref, pnw_ref, wg_ref, wp_ref, fnw_ref,
                o_ref):
    d = x_ref.shape[1]
    y = jnp.concatenate([y3_ref[jj].T for jj in range(y3_ref.shape[0])], axis=0)
    glu = y * jax.nn.sigmoid(jnp.dot(y.astype(bf16), wglu_ref[...], preferred_element_type=f32) + bglu_ref[...])
    z = zs_ref[...].astype(f32)
    ys5 = glu * (z * jax.nn.sigmoid(z))
    h = x_ref[...] + jnp.dot(ys5.astype(bf16), wo_ref[:d, :], preferred_element_type=f32)
    h = h + jnp.dot(yd_ref[...], wo_ref[d:, :], preferred_element_type=f32)
    gate = jax.nn.sigmoid(jnp.dot(_rms(h, pnw_ref[...]).astype(bf16), wg_ref[...], preferred_element_type=f32))
    h = h + jnp.dot(p_ref[...].astype(bf16), wp_ref[...], preferred_element_type=f32) * gate
    o_ref[...] = _rms(h, fnw_ref[...])


def _out_proj(x2, y3, zs, yd, p2, wglu, bglu, wo, pnw, wg, wp, fnw, tb):
    m, d = x2.shape
    tok = lambda i: (i, 0)
    return pl.pallas_call(
        _out_kernel,
        out_shape=jax.ShapeDtypeStruct((m, d), f32),
        grid=(m // tb,),
        in_specs=[pl.BlockSpec((tb, d), tok),
                  pl.BlockSpec((tb // LANES, d, LANES), lambda i: (i, 0, 0)),
                  pl.BlockSpec((tb, d), tok), pl.BlockSpec((tb, d), tok), pl.BlockSpec((tb, p2.shape[1]), tok),
                  _const_spec(wglu.shape), _const_spec(bglu.shape), _const_spec(wo.shape), _const_spec(pnw.shape),
                  _const_spec(wg.shape), _const_spec(wp.shape), _const_spec(fnw.shape)],
        out_specs=pl.BlockSpec((tb, d), tok),
        compiler_params=_cparams("parallel"),
        name="out_proj",
    )(x2, y3, zs, yd, p2, wglu, bglu, wo, pnw, wg, wp, fnw)


def _block(m, want):
    b = want
    while m % b:
        b //= 2
    return b


def kernel(x, p, norm_w, w_in, s5_A_re, s5_A_im, s5_log_dt, s5_B_re, s5_B_im, s5_C_re, s5_C_im, s5_D, s5_w_glu, s5_b_glu, conv_w, conv_b, dt_bias, A_log, ssd_D, ssd_norm_w, w_out, ple_norm_w, w_ple_gate, w_ple_proj, final_norm_w):
    batch, seq, d = x.shape
    assert norm_w.shape[0] == 1, "single-layer problem: the final norm is fused into the layer's last kernel"
    m = batch * seq
    n_heads = dt_bias.shape[1]
    assert seq % (2 * LANES) == 0 and d % LANES == 0
    rows_per_seq = seq // LANES
    assert rows_per_seq & (rows_per_seq - 1) == 0
    x2 = x.reshape(m, d)
    toep, pmat, qt, dec = _s5_params(s5_A_re[0], s5_A_im[0], s5_log_dt[0], s5_B_re[0], s5_B_im[0],
                                     s5_C_re[0], s5_C_im[0])
    wi = w_in[0].astype(bf16)
    wut = jnp.concatenate([wi[:, :d], wi[:, 5 * d:]], axis=1).T
    wr = jnp.concatenate([wi[:, 3 * d:5 * d], wi[:, d:3 * d]], axis=1)
    cw8 = jnp.broadcast_to(conv_w[0][:, None, :], (CONV_WIDTH, 8, conv_w.shape[2]))
    cb8 = jnp.broadcast_to(conv_b[0][None, :], (8, conv_b.shape[1]))
    dtb = jnp.broadcast_to(dt_bias[0][:, None], (n_heads, 2 * LANES))
    u3, zs, zd, xc, dt = _in_proj(x2, norm_w[0][None, :], wut, wr, cw8, cb8, dtb, _block(seq, 512), seq)
    drow = jnp.repeat(s5_D[0].reshape(-1, S5_GROUP_CH), S5_CHUNK, axis=1)[:, None, :]
    y3 = _s5_mix(u3, toep, pmat, qt, dec, drow, rows_per_seq)
    head_of = jnp.arange(d) // SSD_HEAD_DIM
    e2 = (jnp.arange(n_heads)[:, None] == head_of[None, :]).astype(bf16)
    e2 = jnp.concatenate([e2, e2], axis=0)
    tile_of = jnp.arange(n_heads * SSD_CHUNK) // SSD_CHUNK
    sel = (jnp.arange(n_heads)[:, None] == tile_of[None, :]).astype(bf16)
    sel = jnp.concatenate([sel, sel], axis=0)
    alog = jnp.broadcast_to(A_log[0][:, None], (n_heads, SSD_CHUNK))
    yd = _ssd(xc, dt, zd, alog, jnp.repeat(ssd_D[0], SSD_HEAD_DIM)[None, :], ssd_norm_w[0][None, :], e2,
              sel, batch, _block(seq, 512))
    out = _out_proj(x2, y3, zs, yd, p[0].reshape(m, -1), s5_w_glu[0].astype(bf16), s5_b_glu[0][None, :],
                    w_out[0].astype(bf16), ple_norm_w[0][None, :], w_ple_gate[0].astype(bf16),
                    w_ple_proj[0].astype(bf16), final_norm_w[None, :], _block(m, 512))
    return out.reshape(batch, seq, d)
```

```python
import functools

import jax
import jax.numpy as jnp
from jax import lax
from jax.experimental import pallas as pl
from jax.experimental.pallas import tpu as pltpu

f32 = jnp.float32
bf16 = jnp.bfloat16

EPS = 1e-6
LANES = 128
S5_GROUP_CH = 16
S5_STATE = 64
S5_CHUNK = 64
S5_GROUPS_PER_STEP = 2
SSD_HEAD_DIM = 64
SSD_GROUPS = 4
SSD_STATE = 128
SSD_CHUNK = 128
CONV_WIDTH = 4
CONV_ROWS, CONV_COLS = 32, 128
VMEM_LIMIT = 56 * 1024 * 1024


def _cparams(*sem):
    return pltpu.CompilerParams(dimension_semantics=sem, vmem_limit_bytes=VMEM_LIMIT)


def _const_spec(shape):
    nd = len(shape)
    return pl.BlockSpec(shape, lambda *_: (0,) * nd, pipeline_mode=pl.Buffered(1))


def _swap_halves(x):
    return pltpu.roll(x, LANES // 2, x.ndim - 1)


def _s5_params_kernel(are_ref, aim_ref, ldt_ref, bt_ref, ca_ref, cr_ref, ci_ref,
                      toep_ref, p_ref, qt_ref, dec_ref, kv_ref):
    T, H = S5_CHUNK, S5_GROUP_CH
    ar2, ai2 = are_ref[0], aim_ref[0]
    dt = jnp.exp(ldt_ref[0])
    xr, xi = ar2 * dt, ai2 * dt
    lane1 = lax.broadcasted_iota(jnp.int32, (1, LANES), 1)
    sgn = jnp.where(lane1 < T, -1.0, 1.0).astype(f32)

    def cmul(dr, di, x):
        return dr * x + di * (sgn * _swap_halves(x))

    ea = jnp.exp(xr)
    abr, abi = ea * jnp.cos(xi), ea * jnp.sin(xi)
    den = ar2 * ar2 + ai2 * ai2
    nr, ni = abr - 1.0, abi
    qr = (nr * ar2 + ni * ai2) / den
    qi = (ni * ar2 - nr * ai2) / den
    bb = cmul(qr, qi, bt_ref[0])

    def powtab(t):
        e = jnp.exp(t * xr)
        return e * jnp.cos(t * xi), e * jnp.sin(t * xi)

    def zmul(ar_, ai_, br_, bi_):
        return ar_ * br_ - ai_ * bi_, ar_ * bi_ + ai_ * br_

    j8 = lax.broadcasted_iota(jnp.int32, (8, LANES), 0).astype(f32)
    s8r, s8i = powtab(jnp.full((1, LANES), 8.0, f32))
    up, down = [powtab(j8)], [powtab(7.0 - j8)]
    for _ in range(T // 8 - 1):
        up.append(zmul(*up[-1], s8r, s8i))
        down.append(zmul(*down[-1], s8r, s8i))
    pr0, pi0 = (jnp.concatenate([t[k] for t in up], axis=0) for k in range(2))
    pr_rev, pi_rev = (jnp.concatenate([t[k] for t in down[::-1]], axis=0) for k in range(2))
    pr1, pi1 = zmul(pr0, pi0, abr, abi)
    conj = -sgn
    for h in range(H):
        p_ref[0, h * T:(h + 1) * T, :] = cmul(pr_rev, pi_rev, bb[h:h + 1, :]).astype(p_ref.dtype)
        qt_ref[0, h * T:(h + 1) * T, :] = (conj * cmul(pr1, pi1, ca_ref[0, h:h + 1, :])).astype(qt_ref.dtype)

    pr_t, pi_t = s8r, s8i
    for _ in range((T // 8).bit_length() - 1):
        pr_t, pi_t = zmul(pr_t, pi_t, pr_t, pi_t)
    dec_ref[0] = jnp.zeros((8, LANES), f32)
    dec_ref[0, 0:1, :] = pr_t
    dec_ref[0, 1:2, :] = sgn * pi_t

    pows = jnp.where(lax.broadcasted_iota(jnp.int32, (T, LANES), 1) < T, pr0, -pi0)
    zeros = jnp.zeros((T, LANES), f32)
    rhs = jnp.concatenate([jnp.concatenate([pows, zeros], axis=1),
                           jnp.concatenate([zeros, pows], axis=1)], axis=0)
    cb = [cr_ref[0, hp:hp + 1, :] * bb + ci_ref[0, hp:hp + 1, :] * (sgn * _swap_halves(bb)) for hp in range(H)]
    lhs = jnp.concatenate([jnp.concatenate([cb[2 * m], cb[2 * m + 1]], axis=1) for m in range(H // 2)], axis=0)
    kv_ref[...] = lax.dot_general(lhs, rhs, (((1,), (1,)), ((), ())), precision=lax.Precision.HIGHEST,
                                  preferred_element_type=f32)

    row = lax.broadcasted_iota(jnp.int32, (T, LANES), 0)
    col = lax.broadcasted_iota(jnp.int32, (T, LANES), 1)
    causal = (col % T) >= row
    for h in range(H):
        for m in range(H // 2):
            v = jnp.broadcast_to(kv_ref[m * H + h:m * H + h + 1, :], (T, LANES))
            tile = pltpu.roll(v, 0, 1, stride=1, stride_axis=0)
            toep_ref[0, h * T:(h + 1) * T, m * LANES:(m + 1) * LANES] = jnp.where(causal, tile, 0.0).astype(toep_ref.dtype)


def _s5_params(s5_A_re, s5_A_im, s5_log_dt, s5_B_re, s5_B_im, s5_C_re, s5_C_im):
    G, N = s5_A_re.shape
    H, T = S5_GROUP_CH, S5_CHUNK
    dup = lambda a: jnp.concatenate([a, a], axis=-1)
    are = dup(s5_A_re)[:, None, :]
    aim = dup(s5_A_im)[:, None, :]
    ldt = jnp.broadcast_to(s5_log_dt[:, None, None], (G, 1, 2 * N))
    bt = jnp.concatenate([jnp.swapaxes(s5_B_re, 1, 2), jnp.swapaxes(s5_B_im, 1, 2)], axis=-1)
    ca = jnp.concatenate([s5_C_re, s5_C_im], axis=-1)
    cr, ci = dup(s5_C_re), dup(s5_C_im)
    g3 = lambda a: pl.BlockSpec((1,) + a.shape[1:], lambda g: (g, 0, 0))
    ins = (are, aim, ldt, bt, ca, cr, ci)
    return pl.pallas_call(
        _s5_params_kernel,
        out_shape=(jax.ShapeDtypeStruct((G, H * T, H * T), bf16),
                   jax.ShapeDtypeStruct((G, H * T, 2 * N), bf16),
                   jax.ShapeDtypeStruct((G, H * T, 2 * N), bf16),
                   jax.ShapeDtypeStruct((G, 8, 2 * N), f32)),
        grid=(G,),
        in_specs=[g3(a) for a in ins],
        out_specs=(pl.BlockSpec((1, H * T, H * T), lambda g: (g, 0, 0)),
                   pl.BlockSpec((1, H * T, 2 * N), lambda g: (g, 0, 0)),
                   pl.BlockSpec((1, H * T, 2 * N), lambda g: (g, 0, 0)),
                   pl.BlockSpec((1, 8, 2 * N), lambda g: (g, 0, 0))),
        scratch_shapes=[pltpu.VMEM((H // 2 * H, LANES), f32)],
        compiler_params=_cparams("arbitrary"),
        name="s5_params",
    )(*ins)


def _rms(x, w):
    return x * lax.rsqrt(jnp.mean(x * x, axis=-1, keepdims=True) + EPS) * w


def _conv_silu(xpad_ref, cw_ref, cb_ref, xc_ref, c_lo, c_hi):
    tb = xc_ref.shape[0]
    for c0 in range(c_lo, c_hi, CONV_COLS):
        cols = slice(c0, c0 + CONV_COLS)
        sub = lax.broadcasted_iota(jnp.int32, (8, CONV_COLS), 0)
        prev_rolled = [pltpu.roll(xpad_ref[0:8, cols], sh, 0) for sh in range(1, CONV_WIDTH)]
        for r0 in range(0, tb, CONV_ROWS):
            outs = []
            for r in range(r0, r0 + CONV_ROWS, 8):
                cur = xpad_ref[8 + r:16 + r, cols]
                acc = cb_ref[:, cols] + cw_ref[CONV_WIDTH - 1, :, cols] * cur
                cur_rolled = [pltpu.roll(cur, sh, 0) for sh in range(1, CONV_WIDTH)]
                for sh in range(1, CONV_WIDTH):
                    acc = acc + cw_ref[CONV_WIDTH - 1 - sh, :, cols] * jnp.where(sub < sh, prev_rolled[sh - 1],
                                                                                  cur_rolled[sh - 1])
                outs.append(acc * jax.nn.sigmoid(acc))
                prev_rolled = cur_rolled
            xc_ref[r0:r0 + CONV_ROWS, cols] = jnp.concatenate(outs, axis=0).astype(xc_ref.dtype)


def _in_proj_kernel(x_ref, nw_ref, wu_ref, wr_ref, cw_ref, cb_ref, dtb_ref,
                    u3_ref, zs_ref, zd_ref, xc_ref, dt_ref, xpad_ref, hn_ref, wut_ref, *, blocks_per_seq):
    tb, d = x_ref.shape
    i = pl.program_id(0)
    n_cm = d + dtb_ref.shape[0]

    @pl.when(i == 0)
    def _():
        xpad_ref[tb:tb + 8, :] = jnp.zeros((8, xpad_ref.shape[1]), f32)
        wut_ref[...] = wu_ref[...].T

    @pl.when(i >= 0)
    def _():
        hn_ref[...] = _rms(x_ref[...], nw_ref[...]).astype(bf16)
        tail = xpad_ref[tb:tb + 8, :]
        xpad_ref[0:8, :] = jnp.where(i % blocks_per_seq != 0, tail, 0.0)

    hn = hn_ref[...]

    def z_cols(lo, hi, ref):
        ref[...] = jnp.dot(hn, wr_ref[:, lo:hi], preferred_element_type=f32).astype(ref.dtype)

    def u3_rows(jj):
        v = lax.dot_general(wut_ref[0:n_cm, :], hn[jj * 2 * LANES:(jj + 1) * 2 * LANES, :],
                            (((1,), (1,)), ((), ())), preferred_element_type=f32)
        u3_ref[2 * jj] = v[:d, :LANES]
        u3_ref[2 * jj + 1] = v[:d, LANES:]
        dt_ref[:, jj * 2 * LANES:(jj + 1) * 2 * LANES] = jax.nn.softplus(v[d:, :] + dtb_ref[...])

    n_slabs = 4
    slab = 2 * d // n_slabs
    for k in range(n_slabs):
        xpad_ref[8:8 + tb, k * slab:(k + 1) * slab] = jnp.dot(hn, wr_ref[:, k * slab:(k + 1) * slab],
                                                              preferred_element_type=f32)
    for jj in range(tb // (2 * LANES)):
        u3_rows(jj)
    for k in range(n_slabs):
        _conv_silu(xpad_ref, cw_ref, cb_ref, xc_ref, k * slab, (k + 1) * slab)
    z_cols(2 * d, 3 * d, zs_ref)
    z_cols(3 * d, 4 * d, zd_ref)


def _in_proj(x2, norm_w, wu, wr, cw, cb, dtb, tb, seq):
    m, d = x2.shape
    nr = wr.shape[1]
    tok = lambda i: (i, 0)
    return pl.pallas_call(
        functools.partial(_in_proj_kernel, blocks_per_seq=seq // tb),
        out_shape=(jax.ShapeDtypeStruct((m // LANES, d, LANES), f32),
                   jax.ShapeDtypeStruct((m, d), bf16),
                   jax.ShapeDtypeStruct((m, d), bf16),
                   jax.ShapeDtypeStruct((m, 2 * d), bf16),
                   jax.ShapeDtypeStruct((dtb.shape[0], m), f32)),
        grid=(m // tb,),
        in_specs=[pl.BlockSpec((tb, d), tok),
                  _const_spec((1, d)), _const_spec(wu.shape), _const_spec((d, nr)),
                  _const_spec(cw.shape), _const_spec(cb.shape), _const_spec(dtb.shape)],
        out_specs=(pl.BlockSpec((tb // LANES, d, LANES), lambda i: (i, 0, 0)),
                   pl.BlockSpec((tb, d), tok), pl.BlockSpec((tb, d), tok),
                   pl.BlockSpec((tb, 2 * d), tok), pl.BlockSpec((dtb.shape[0], tb), lambda i: (0, i))),
        scratch_shapes=[pltpu.VMEM((tb + 8, 2 * d), f32), pltpu.VMEM((tb, d), bf16),
                        pltpu.VMEM((wu.shape[1], d), bf16)],
        compiler_params=_cparams("arbitrary"),
        name="in_proj",
    )(x2, norm_w, wu, wr, cw, cb, dtb)


class _CopyGroup:
    def __init__(self, copies):
        self.copies = copies

    def start(self):
        for cp in self.copies:
            cp.start()

    def wait(self):
        for cp in self.copies:
            cp.wait()


def _s5_mix_kernel(u_hbm, toep_ref, p_ref, qt_ref, dec_ref, d_ref, o_hbm, ubuf, ybuf, sem_in, sem_out, *,
                   rows_per_seq):
    g = pl.program_id(0)
    n_g = pl.num_programs(0)
    slot = g % 2
    HC = ubuf.shape[1]

    def in_copy(step, s):
        return _CopyGroup([pltpu.make_async_copy(u_hbm.at[:, step * HC + h, :], ubuf.at[s, h], sem_in.at[s])
                           for h in range(HC)])

    def out_copy(step, s):
        return _CopyGroup([pltpu.make_async_copy(ybuf.at[s, h], o_hbm.at[:, step * HC + h, :], sem_out.at[s])
                           for h in range(HC)])

    @pl.when(g == 0)
    def _():
        in_copy(0, 0).start()

    @pl.when(g + 1 < n_g)
    def _():
        in_copy(g + 1, 1 - slot).start()

    in_copy(g, slot).wait()

    @pl.when(g >= 2)
    def _():
        out_copy(g - 2, slot).wait()

    for sg in range(HC // S5_GROUP_CH):
        _s5_group(ubuf.at[slot], ybuf.at[slot], sg * S5_GROUP_CH, toep_ref[sg], p_ref[sg], qt_ref[sg], dec_ref[sg],
                  d_ref[sg], rows_per_seq)

    out_copy(g, slot).start()

    @pl.when(g == n_g - 1)
    def _():
        @pl.when(n_g >= 2)
        def _():
            out_copy(g - 1, 1 - slot).wait()
        out_copy(g, slot).wait()


def _s5_group(u_ref, o_ref, c0, toep, pmat, qt, dec, drow, rows_per_seq):
    J = u_ref.shape[1]
    H = S5_GROUP_CH
    half = LANES // 2
    lo = lax.broadcasted_iota(jnp.int32, (J, LANES), 1) < half
    e0, e1 = [], []
    for m in range(H // 2):
        a, b = u_ref[c0 + 2 * m], u_ref[c0 + 2 * m + 1]
        e0.append(jnp.where(lo, a, _swap_halves(b)))
        e1.append(jnp.where(lo, _swap_halves(a), b))
    e = jnp.concatenate([jnp.concatenate(e0, axis=1), jnp.concatenate(e1, axis=1)], axis=0)
    eb = e.astype(bf16)
    y = jnp.dot(eb, toep, preferred_element_type=f32)
    s = jnp.dot(eb, pmat, preferred_element_type=f32)
    s0, s1 = s[:J], s[J:]
    ar, ai = dec[0:1, :], dec[1:2, :]

    def cm(cr, ci, x):
        return cr * x + ci * _swap_halves(x)

    def csq(cr, ci):
        return cr * cr - ci * ci, 2.0 * cr * ci

    v = cm(ar, ai, s0) + s1
    k_idx = lax.broadcasted_iota(jnp.int32, (J, LANES), 0) % rows_per_seq
    w = v
    dr, di = csq(ar, ai)
    step = 1
    while step < rows_per_seq:
        sh = jnp.where(k_idx >= step, pltpu.roll(w, step, 0), 0.0)
        w = w + cm(dr, di, sh)
        dr, di = csq(dr, di)
        step *= 2
    h0 = jnp.where(k_idx >= 1, pltpu.roll(w, 1, 0), 0.0)
    h1 = cm(ar, ai, h0) + s0
    hb = jnp.concatenate([h0, h1], axis=0).astype(bf16)
    y = y + lax.dot_general(hb, qt, (((1,), (1,)), ((), ())), preferred_element_type=f32)
    y = jax.nn.gelu(y + e * drow)
    for m in range(H // 2):
        y0 = y[:J, m * LANES:(m + 1) * LANES]
        y1 = y[J:, m * LANES:(m + 1) * LANES]
        o_ref[c0 + 2 * m] = jnp.where(lo, y0, _swap_halves(y1))
        o_ref[c0 + 2 * m + 1] = jnp.where(lo, _swap_halves(y0), y1)


def _s5_mix(u3, toep, p, qt, dec, drow, rows_per_seq):
    J = u3.shape[0]
    G, w, _ = toep.shape
    gps = S5_GROUPS_PER_STEP
    hc = gps * S5_GROUP_CH
    n2 = p.shape[2]
    per_step = lambda a: pl.BlockSpec((gps,) + a.shape[1:], lambda g: (g, 0, 0))
    return pl.pallas_call(
        functools.partial(_s5_mix_kernel, rows_per_seq=rows_per_seq),
        out_shape=jax.ShapeDtypeStruct(u3.shape, f32),
        grid=(G // gps,),
        in_specs=[pl.BlockSpec(memory_space=pl.ANY),
                  per_step(toep), per_step(p), per_step(qt), per_step(dec), per_step(drow)],
        out_specs=pl.BlockSpec(memory_space=pl.ANY),
        scratch_shapes=[pltpu.VMEM((2, hc, J, LANES), f32), pltpu.VMEM((2, hc, J, LANES), f32),
                        pltpu.SemaphoreType.DMA((2,)), pltpu.SemaphoreType.DMA((2,))],
        compiler_params=_cparams("arbitrary"),
        name="s5_mix",
    )(u3, toep, p, qt, dec, drow)


def _split_bf16_rows(v):
    hi = v.astype(bf16)
    lo = (v - hi.astype(f32)).astype(bf16)
    return jnp.concatenate([hi, lo], axis=0)


def _ssd_kernel(xc_ref, dt_ref, zd_ref, alog_ref, dexp_ref, nw_ref, e2_ref, sel_ref, o_ref, state_ref):
    ts = xc_ref.shape[0]
    d = o_ref.shape[1]
    T, N, P = SSD_CHUNK, SSD_STATE, SSD_HEAD_DIM
    gw = d // SSD_GROUPS
    hpg = gw // P
    n_chunks = ts // T

    @pl.when(pl.program_id(1) == 0)
    def _():
        state_ref[...] = jnp.zeros_like(state_ref)

    dtt = dt_ref[...]
    neg_a = -jnp.exp(alog_ref[...])
    li = lax.broadcasted_iota(jnp.int32, (T, T), 0)
    si = lax.broadcasted_iota(jnp.int32, (T, T), 1)
    causal = li >= si
    triu = (li <= si).astype(f32)
    acs_c, w_c = [], []
    for c in range(n_chunks):
        dtc = dtt[:, c * T:(c + 1) * T]
        acs = jnp.dot(dtc * neg_a, triu, precision=lax.Precision.HIGHEST, preferred_element_type=f32)
        acs_c.append(acs)
        w_c.append(jnp.exp(acs[:, T - 1:T] - acs) * dtc)
    acs_t = jnp.concatenate(acs_c, axis=1)
    we_parts = _split_bf16_rows(jnp.concatenate(w_c + [jnp.exp(acs_t)], axis=1))
    ex = lax.dot_general(we_parts, e2_ref[...], (((0,), (0,)), ((), ())), preferred_element_type=f32)
    w_x, e_x = ex[:ts], ex[ts:]
    a_parts = _split_bf16_rows(acs_t)
    n_h = acs_t.shape[0]
    acs_r = a_parts[:n_h].astype(f32) + a_parts[n_h:].astype(f32)
    colb = lax.dot_general(a_parts, sel_ref[...], (((0,), (0,)), ((), ())),
                           preferred_element_type=f32)
    head_of_lane = lax.broadcasted_iota(jnp.int32, (T, gw), 1) // P
    for c in range(n_chunks):
        sl = slice(c * T, (c + 1) * T)
        a_cst = acs_r[:, sl]
        ys = []
        for q in range(SSD_GROUPS):
            cs = slice(q * gw, (q + 1) * gw)
            bq = xc_ref[sl, d + q * N:d + (q + 1) * N]
            cq = xc_ref[sl, d + SSD_GROUPS * N + q * N:d + SSD_GROUPS * N + (q + 1) * N]
            xq = xc_ref[sl, cs]
            g = lax.dot_general(cq, bq, (((1,), (1,)), ((), ())), preferred_element_type=f32)
            scores, xbd = [], []
            for hh in range(hpg):
                h = q * hpg + hh
                diff = colb[sl, h * T:(h + 1) * T] - a_cst[h:h + 1, :]
                lm = jnp.exp(jnp.where(causal, diff, -1e30))
                scores.append((g * lm * dtt[h:h + 1, sl]).astype(bf16))
                xbd.append(jnp.where(head_of_lane == hh, xq, jnp.zeros_like(xq)))
            y_diag = jnp.dot(jnp.concatenate(scores, axis=1), jnp.concatenate(xbd, axis=0),
                             preferred_element_type=f32)
            st = state_ref[:, cs]
            y_off = jnp.dot(cq, st.astype(bf16), preferred_element_type=f32) * e_x[sl, cs]
            xw = xq * w_x[sl, cs].astype(bf16)
            s_new = lax.dot_general(bq, xw, (((0,), (0,)), ((), ())), preferred_element_type=f32)
            state_ref[:, cs] = st * e_x[(c + 1) * T - 1:(c + 1) * T, cs] + s_new
            ys.append(y_diag + y_off)
        y = jnp.concatenate(ys, axis=1) + dexp_ref[...] * xc_ref[sl, :d].astype(f32)
        z = zd_ref[sl, :].astype(f32)
        gt = y * (z * jax.nn.sigmoid(z))
        gn = []
        for q in range(SSD_GROUPS):
            gq = gt[:, q * gw:(q + 1) * gw]
            gn.append(gq * lax.rsqrt(jnp.sum(gq * gq, axis=-1, keepdims=True) * (1.0 / gw) + EPS))
        o_ref[sl, :] = (jnp.concatenate(gn, axis=1) * nw_ref[...]).astype(o_ref.dtype)


def _ssd(xc, dt, zd, alog, dexp, nw, e2, sel, batch, ts):
    m, cdim = xc.shape
    d = zd.shape[1]
    nb = m // batch // ts
    tok = lambda b, i: (b * nb + i, 0)
    return pl.pallas_call(
        _ssd_kernel,
        out_shape=jax.ShapeDtypeStruct((m, d), bf16),
        grid=(batch, nb),
        in_specs=[pl.BlockSpec((ts, cdim), tok), pl.BlockSpec((dt.shape[0], ts), lambda b, i: (0, b * nb + i)),
                  pl.BlockSpec((ts, d), tok), _const_spec(alog.shape), _const_spec(dexp.shape), _const_spec(nw.shape), _const_spec(e2.shape),
                  _const_spec(sel.shape)],
        out_specs=pl.BlockSpec((ts, d), tok),
        scratch_shapes=[pltpu.VMEM((SSD_STATE, d), f32)],
        compiler_params=_cparams("arbitrary", "arbitrary"),
        name="ssd",
    )(xc, dt, zd, alog, dexp, nw, e2, sel)


def _out_kernel(x_ref, y3_ref, zs_ref, yd_ref, p_ref, wglu_ref, bglu_ref, wo_ref, pnw_ref, wg_ref, wp_ref, fnw_ref,
                o_ref):
    d = x_ref.shape[1]
    y = jnp.concatenate([y3_ref[jj].T for jj in range(y3_ref.shape[0])], axis=0)
    glu = y * jax.nn.sigmoid(jnp.dot(y.astype(bf16), wglu_ref[...], preferred_element_type=f32) + bglu_ref[...])
    z = zs_ref[...].astype(f32)
    ys5 = glu * (z * jax.nn.sigmoid(z))
    h = x_ref[...] + jnp.dot(ys5.astype(bf16), wo_ref[:d, :], preferred_element_type=f32)
    h = h + jnp.dot(yd_ref[...], wo_ref[d:, :], preferred_element_type=f32)
    gate = jax.nn.sigmoid(jnp.dot(_rms(h, pnw_ref[...]).astype(bf16), wg_ref[...], preferred_element_type=f32))
    h = h + jnp.dot(p_ref[...].astype(bf16), wp_ref[...], preferred_element_type=f32) * gate
    o_ref[...] = _rms(h, fnw_ref[...])


def _out_proj(x2, y3, zs, yd, p2, wglu, bglu, wo, pnw, wg, wp, fnw, tb):
    m, d = x2.shape
    tok = lambda i: (i, 0)
    return pl.pallas_call(
        _out_kernel,
        out_shape=jax.ShapeDtypeStruct((m, d), f32),
        grid=(m // tb,),
        in_specs=[pl.BlockSpec((tb, d), tok),
                  pl.BlockSpec((tb // LANES, d, LANES), lambda i: (i, 0, 0)),
                  pl.BlockSpec((tb, d), tok), pl.BlockSpec((tb, d), tok), pl.BlockSpec((tb, p2.shape[1]), tok),
                  _const_spec(wglu.shape), _const_spec(bglu.shape), _const_spec(wo.shape), _const_spec(pnw.shape),
                  _const_spec(wg.shape), _const_spec(wp.shape), _const_spec(fnw.shape)],
        out_specs=pl.BlockSpec((tb, d), tok),
        compiler_params=_cparams("parallel"),
        name="out_proj",
    )(x2, y3, zs, yd, p2, wglu, bglu, wo, pnw, wg, wp, fnw)


def _block(m, want):
    b = want
    while m % b:
        b //= 2
    return b


def kernel(x, p, norm_w, w_in, s5_A_re, s5_A_im, s5_log_dt, s5_B_re, s5_B_im, s5_C_re, s5_C_im, s5_D, s5_w_glu, s5_b_glu, conv_w, conv_b, dt_bias, A_log, ssd_D, ssd_norm_w, w_out, ple_norm_w, w_ple_gate, w_ple_proj, final_norm_w):
    batch, seq, d = x.shape
    assert norm_w.shape[0] == 1, "single-layer problem: the final norm is fused into the layer's last kernel"
    m = batch * seq
    n_heads = dt_bias.shape[1]
    assert seq % (2 * LANES) == 0 and d % LANES == 0
    rows_per_seq = seq // LANES
    assert rows_per_seq & (rows_per_seq - 1) == 0
    x2 = x.reshape(m, d)
    toep, pmat, qt, dec = _s5_params(s5_A_re[0], s5_A_im[0], s5_log_dt[0], s5_B_re[0], s5_B_im[0],
                                     s5_C_re[0], s5_C_im[0])
    wi = w_in[0].astype(bf16)
    wu = jnp.pad(jnp.concatenate([wi[:, :d], wi[:, 5 * d:]], axis=1), ((0, 0), (0, LANES - n_heads)))
    wr = jnp.concatenate([wi[:, 3 * d:5 * d], wi[:, d:3 * d]], axis=1)
    cw8 = jnp.broadcast_to(conv_w[0][:, None, :], (CONV_WIDTH, 8, conv_w.shape[2]))
    cb8 = jnp.broadcast_to(conv_b[0][None, :], (8, conv_b.shape[1]))
    dtb = jnp.broadcast_to(dt_bias[0][:, None], (n_heads, 2 * LANES))
    u3, zs, zd, xc, dt = _in_proj(x2, norm_w[0][None, :], wu, wr, cw8, cb8, dtb, _block(seq, 512), seq)
    drow = jnp.repeat(s5_D[0].reshape(-1, S5_GROUP_CH), S5_CHUNK, axis=1)[:, None, :]
    y3 = _s5_mix(u3, toep, pmat, qt, dec, drow, rows_per_seq)
    head_of = jnp.arange(d) // SSD_HEAD_DIM
    e2 = (jnp.arange(n_heads)[:, None] == head_of[None, :]).astype(bf16)
    e2 = jnp.concatenate([e2, e2], axis=0)
    tile_of = jnp.arange(n_heads * SSD_CHUNK) // SSD_CHUNK
    sel = (jnp.arange(n_heads)[:, None] == tile_of[None, :]).astype(bf16)
    sel = jnp.concatenate([sel, sel], axis=0)
    alog = jnp.broadcast_to(A_log[0][:, None], (n_heads, SSD_CHUNK))
    yd = _ssd(xc, dt, zd, alog, jnp.repeat(ssd_D[0], SSD_HEAD_DIM)[None, :], ssd_norm_w[0][None, :], e2,
              sel, batch, _block(seq, 512))
    out = _out_proj(x2, y3, zs, yd, p[0].reshape(m, -1), s5_w_glu[0].astype(bf16), s5_b_glu[0][None, :],
                    w_out[0].astype(bf16), ple_norm_w[0][None, :], w_ple_gate[0].astype(bf16),
                    w_ple_proj[0].astype(bf16), final_norm_w[None, :], _block(m, 512))
    return out.reshape(batch, seq, d)
```

```python
import functools

import jax
import jax.numpy as jnp
from jax import lax
from jax.experimental import pallas as pl
from jax.experimental.pallas import tpu as pltpu

f32 = jnp.float32
bf16 = jnp.bfloat16

EPS = 1e-6
LANES = 128
S5_GROUP_CH = 16
S5_STATE = 64
S5_CHUNK = 64
S5_GROUPS_PER_STEP = 2
SSD_HEAD_DIM = 64
SSD_GROUPS = 4
SSD_STATE = 128
SSD_CHUNK = 128
CONV_WIDTH = 4
CONV_ROWS, CONV_COLS = 32, 128
VMEM_LIMIT = 56 * 1024 * 1024


def _cparams(*sem):
    return pltpu.CompilerParams(dimension_semantics=sem, vmem_limit_bytes=VMEM_LIMIT)


def _const_spec(shape):
    nd = len(shape)
    return pl.BlockSpec(shape, lambda *_: (0,) * nd, pipeline_mode=pl.Buffered(1))


def _swap_halves(x):
    return pltpu.roll(x, LANES // 2, x.ndim - 1)


def _s5_params_kernel(are_ref, aim_ref, ldt_ref, bt_ref, ca_ref, cr_ref, ci_ref,
                      toep_ref, p_ref, qt_ref, dec_ref, kv_ref):
    T, H = S5_CHUNK, S5_GROUP_CH
    ar2, ai2 = are_ref[0], aim_ref[0]
    dt = jnp.exp(ldt_ref[0])
    xr, xi = ar2 * dt, ai2 * dt
    lane1 = lax.broadcasted_iota(jnp.int32, (1, LANES), 1)
    sgn = jnp.where(lane1 < T, -1.0, 1.0).astype(f32)

    def cmul(dr, di, x):
        return dr * x + di * (sgn * _swap_halves(x))

    ea = jnp.exp(xr)
    abr, abi = ea * jnp.cos(xi), ea * jnp.sin(xi)
    den = ar2 * ar2 + ai2 * ai2
    nr, ni = abr - 1.0, abi
    qr = (nr * ar2 + ni * ai2) / den
    qi = (ni * ar2 - nr * ai2) / den
    bb = cmul(qr, qi, bt_ref[0])

    def powtab(t):
        e = jnp.exp(t * xr)
        return e * jnp.cos(t * xi), e * jnp.sin(t * xi)

    def zmul(ar_, ai_, br_, bi_):
        return ar_ * br_ - ai_ * bi_, ar_ * bi_ + ai_ * br_

    j8 = lax.broadcasted_iota(jnp.int32, (8, LANES), 0).astype(f32)
    s8r, s8i = powtab(jnp.full((1, LANES), 8.0, f32))
    up, down = [powtab(j8)], [powtab(7.0 - j8)]
    for _ in range(T // 8 - 1):
        up.append(zmul(*up[-1], s8r, s8i))
        down.append(zmul(*down[-1], s8r, s8i))
    pr0, pi0 = (jnp.concatenate([t[k] for t in up], axis=0) for k in range(2))
    pr_rev, pi_rev = (jnp.concatenate([t[k] for t in down[::-1]], axis=0) for k in range(2))
    pr1, pi1 = zmul(pr0, pi0, abr, abi)
    conj = -sgn
    for h in range(H):
        p_ref[0, h * T:(h + 1) * T, :] = cmul(pr_rev, pi_rev, bb[h:h + 1, :]).astype(p_ref.dtype)
        qt_ref[0, h * T:(h + 1) * T, :] = (conj * cmul(pr1, pi1, ca_ref[0, h:h + 1, :])).astype(qt_ref.dtype)

    pr_t, pi_t = s8r, s8i
    for _ in range((T // 8).bit_length() - 1):
        pr_t, pi_t = zmul(pr_t, pi_t, pr_t, pi_t)
    dec_ref[0] = jnp.zeros((8, LANES), f32)
    dec_ref[0, 0:1, :] = pr_t
    dec_ref[0, 1:2, :] = sgn * pi_t

    pows = jnp.where(lax.broadcasted_iota(jnp.int32, (T, LANES), 1) < T, pr0, -pi0)
    zeros = jnp.zeros((T, LANES), f32)
    rhs = jnp.concatenate([jnp.concatenate([pows, zeros], axis=1),
                           jnp.concatenate([zeros, pows], axis=1)], axis=0)
    cb = [cr_ref[0, hp:hp + 1, :] * bb + ci_ref[0, hp:hp + 1, :] * (sgn * _swap_halves(bb)) for hp in range(H)]
    lhs = jnp.concatenate([jnp.concatenate([cb[2 * m], cb[2 * m + 1]], axis=1) for m in range(H // 2)], axis=0)
    kv_ref[...] = lax.dot_general(lhs, rhs, (((1,), (1,)), ((), ())), precision=lax.Precision.HIGHEST,
                                  preferred_element_type=f32)

    row = lax.broadcasted_iota(jnp.int32, (T, LANES), 0)
    col = lax.broadcasted_iota(jnp.int32, (T, LANES), 1)
    causal = (col % T) >= row
    for h in range(H):
        for m in range(H // 2):
            v = jnp.broadcast_to(kv_ref[m * H + h:m * H + h + 1, :], (T, LANES))
            tile = pltpu.roll(v, 0, 1, stride=1, stride_axis=0)
            toep_ref[0, h * T:(h + 1) * T, m * LANES:(m + 1) * LANES] = jnp.where(causal, tile, 0.0).astype(toep_ref.dtype)


def _s5_params(s5_A_re, s5_A_im, s5_log_dt, s5_B_re, s5_B_im, s5_C_re, s5_C_im):
    G, N = s5_A_re.shape
    H, T = S5_GROUP_CH, S5_CHUNK
    dup = lambda a: jnp.concatenate([a, a], axis=-1)
    are = dup(s5_A_re)[:, None, :]
    aim = dup(s5_A_im)[:, None, :]
    ldt = jnp.broadcast_to(s5_log_dt[:, None, None], (G, 1, 2 * N))
    bt = jnp.concatenate([jnp.swapaxes(s5_B_re, 1, 2), jnp.swapaxes(s5_B_im, 1, 2)], axis=-1)
    ca = jnp.concatenate([s5_C_re, s5_C_im], axis=-1)
    cr, ci = dup(s5_C_re), dup(s5_C_im)
    g3 = lambda a: pl.BlockSpec((1,) + a.shape[1:], lambda g: (g, 0, 0))
    ins = (are, aim, ldt, bt, ca, cr, ci)
    return pl.pallas_call(
        _s5_params_kernel,
        out_shape=(jax.ShapeDtypeStruct((G, H * T, H * T), bf16),
                   jax.ShapeDtypeStruct((G, H * T, 2 * N), bf16),
                   jax.ShapeDtypeStruct((G, H * T, 2 * N), bf16),
                   jax.ShapeDtypeStruct((G, 8, 2 * N), f32)),
        grid=(G,),
        in_specs=[g3(a) for a in ins],
        out_specs=(pl.BlockSpec((1, H * T, H * T), lambda g: (g, 0, 0)),
                   pl.BlockSpec((1, H * T, 2 * N), lambda g: (g, 0, 0)),
                   pl.BlockSpec((1, H * T, 2 * N), lambda g: (g, 0, 0)),
                   pl.BlockSpec((1, 8, 2 * N), lambda g: (g, 0, 0))),
        scratch_shapes=[pltpu.VMEM((H // 2 * H, LANES), f32)],
        compiler_params=_cparams("arbitrary"),
        name="s5_params",
    )(*ins)


def _rms(x, w):
    return x * lax.rsqrt(jnp.mean(x * x, axis=-1, keepdims=True) + EPS) * w


def _conv_silu(xpad_ref, cw_ref, cb_ref, xc_ref, c_lo, c_hi):
    tb = xc_ref.shape[0]
    for c0 in range(c_lo, c_hi, CONV_COLS):
        cols = slice(c0, c0 + CONV_COLS)
        sub = lax.broadcasted_iota(jnp.int32, (8, CONV_COLS), 0)
        prev_rolled = [pltpu.roll(xpad_ref[0:8, cols], sh, 0) for sh in range(1, CONV_WIDTH)]
        for r0 in range(0, tb, CONV_ROWS):
            outs = []
            for r in range(r0, r0 + CONV_ROWS, 8):
                cur = xpad_ref[8 + r:16 + r, cols]
                acc = cb_ref[:, cols] + cw_ref[CONV_WIDTH - 1, :, cols] * cur
                cur_rolled = [pltpu.roll(cur, sh, 0) for sh in range(1, CONV_WIDTH)]
                for sh in range(1, CONV_WIDTH):
                    acc = acc + cw_ref[CONV_WIDTH - 1 - sh, :, cols] * jnp.where(sub < sh, prev_rolled[sh - 1],
                                                                                  cur_rolled[sh - 1])
                outs.append(acc * jax.nn.sigmoid(acc))
                prev_rolled = cur_rolled
            xc_ref[r0:r0 + CONV_ROWS, cols] = jnp.concatenate(outs, axis=0).astype(xc_ref.dtype)


def _in_proj_kernel(x_ref, nw_ref, wu_ref, wr_ref, cw_ref, cb_ref, dtb_ref,
                    u3_ref, zs_ref, zd_ref, xc_ref, dt_ref, xpad_ref, hn_ref, wut_ref, *, blocks_per_seq):
    tb, d = x_ref.shape
    i = pl.program_id(0)
    n_cm = d + dtb_ref.shape[0]

    @pl.when(i == 0)
    def _():
        xpad_ref[tb:tb + 8, :] = jnp.zeros((8, xpad_ref.shape[1]), f32)
        wut_ref[...] = wu_ref[...].T

    @pl.when(i >= 0)
    def _():
        hn_ref[...] = _rms(x_ref[...], nw_ref[...]).astype(bf16)
        tail = xpad_ref[tb:tb + 8, :]
        xpad_ref[0:8, :] = jnp.where(i % blocks_per_seq != 0, tail, 0.0)

    hn = hn_ref[...]

    def z_cols(lo, hi, ref):
        ref[...] = jnp.dot(hn, wr_ref[:, lo:hi], preferred_element_type=f32).astype(ref.dtype)

    def u3_rows(jj):
        v = lax.dot_general(wut_ref[0:n_cm, :], hn[jj * 2 * LANES:(jj + 1) * 2 * LANES, :],
                            (((1,), (1,)), ((), ())), preferred_element_type=f32)
        u3_ref[2 * jj] = v[:d, :LANES]
        u3_ref[2 * jj + 1] = v[:d, LANES:]
        dt_ref[:, jj * 2 * LANES:(jj + 1) * 2 * LANES] = jax.nn.softplus(v[d:, :] + dtb_ref[...])

    n_slabs = 4
    slab = 2 * d // n_slabs
    for k in range(n_slabs):
        xpad_ref[8:8 + tb, k * slab:(k + 1) * slab] = jnp.dot(hn, wr_ref[:, 2 * d + k * slab:2 * d + (k + 1) * slab],
                                                              preferred_element_type=f32)
    for jj in range(tb // (2 * LANES)):
        u3_rows(jj)
    for k in range(n_slabs):
        _conv_silu(xpad_ref, cw_ref, cb_ref, xc_ref, k * slab, (k + 1) * slab)
    z_cols(0, d, zs_ref)
    z_cols(d, 2 * d, zd_ref)


def _in_proj(x2, norm_w, wu, wr, cw, cb, dtb, tb, seq):
    m, d = x2.shape
    nr = wr.shape[1]
    tok = lambda i: (i, 0)
    return pl.pallas_call(
        functools.partial(_in_proj_kernel, blocks_per_seq=seq // tb),
        out_shape=(jax.ShapeDtypeStruct((m // LANES, d, LANES), f32),
                   jax.ShapeDtypeStruct((m, d), bf16),
                   jax.ShapeDtypeStruct((m, d), bf16),
                   jax.ShapeDtypeStruct((m, 2 * d), bf16),
                   jax.ShapeDtypeStruct((dtb.shape[0], m), f32)),
        grid=(m // tb,),
        in_specs=[pl.BlockSpec((tb, d), tok),
                  _const_spec((1, d)), _const_spec(wu.shape), _const_spec((d, nr)),
                  _const_spec(cw.shape), _const_spec(cb.shape), _const_spec(dtb.shape)],
        out_specs=(pl.BlockSpec((tb // LANES, d, LANES), lambda i: (i, 0, 0)),
                   pl.BlockSpec((tb, d), tok), pl.BlockSpec((tb, d), tok),
                   pl.BlockSpec((tb, 2 * d), tok), pl.BlockSpec((dtb.shape[0], tb), lambda i: (0, i))),
        scratch_shapes=[pltpu.VMEM((tb + 8, 2 * d), f32), pltpu.VMEM((tb, d), bf16),
                        pltpu.VMEM((wu.shape[1], d), bf16)],
        compiler_params=_cparams("arbitrary"),
        name="in_proj",
    )(x2, norm_w, wu, wr, cw, cb, dtb)


class _CopyGroup:
    def __init__(self, copies):
        self.copies = copies

    def start(self):
        for cp in self.copies:
            cp.start()

    def wait(self):
        for cp in self.copies:
            cp.wait()


def _s5_mix_kernel(u_hbm, toep_ref, p_ref, qt_ref, dec_ref, d_ref, o_hbm, ubuf, ybuf, sem_in, sem_out, *,
                   rows_per_seq):
    g = pl.program_id(0)
    n_g = pl.num_programs(0)
    slot = g % 2
    HC = ubuf.shape[1]

    def in_copy(step, s):
        return _CopyGroup([pltpu.make_async_copy(u_hbm.at[:, step * HC + h, :], ubuf.at[s, h], sem_in.at[s])
                           for h in range(HC)])

    def out_copy(step, s):
        return _CopyGroup([pltpu.make_async_copy(ybuf.at[s, h], o_hbm.at[:, step * HC + h, :], sem_out.at[s])
                           for h in range(HC)])

    @pl.when(g == 0)
    def _():
        in_copy(0, 0).start()

    @pl.when(g + 1 < n_g)
    def _():
        in_copy(g + 1, 1 - slot).start()

    in_copy(g, slot).wait()

    @pl.when(g >= 2)
    def _():
        out_copy(g - 2, slot).wait()

    for sg in range(HC // S5_GROUP_CH):
        _s5_group(ubuf.at[slot], ybuf.at[slot], sg * S5_GROUP_CH, toep_ref[sg], p_ref[sg], qt_ref[sg], dec_ref[sg],
                  d_ref[sg], rows_per_seq)

    out_copy(g, slot).start()

    @pl.when(g == n_g - 1)
    def _():
        @pl.when(n_g >= 2)
        def _():
            out_copy(g - 1, 1 - slot).wait()
        out_copy(g, slot).wait()


def _s5_group(u_ref, o_ref, c0, toep, pmat, qt, dec, drow, rows_per_seq):
    J = u_ref.shape[1]
    H = S5_GROUP_CH
    half = LANES // 2
    lo = lax.broadcasted_iota(jnp.int32, (J, LANES), 1) < half
    e0, e1 = [], []
    for m in range(H // 2):
        a, b = u_ref[c0 + 2 * m], u_ref[c0 + 2 * m + 1]
        e0.append(jnp.where(lo, a, _swap_halves(b)))
        e1.append(jnp.where(lo, _swap_halves(a), b))
    e = jnp.concatenate([jnp.concatenate(e0, axis=1), jnp.concatenate(e1, axis=1)], axis=0)
    eb = e.astype(bf16)
    y = jnp.dot(eb, toep, preferred_element_type=f32)
    s = jnp.dot(eb, pmat, preferred_element_type=f32)
    s0, s1 = s[:J], s[J:]
    ar, ai = dec[0:1, :], dec[1:2, :]

    def cm(cr, ci, x):
        return cr * x + ci * _swap_halves(x)

    def csq(cr, ci):
        return cr * cr - ci * ci, 2.0 * cr * ci

    v = cm(ar, ai, s0) + s1
    k_idx = lax.broadcasted_iota(jnp.int32, (J, LANES), 0) % rows_per_seq
    w = v
    dr, di = csq(ar, ai)
    step = 1
    while step < rows_per_seq:
        sh = jnp.where(k_idx >= step, pltpu.roll(w, step, 0), 0.0)
        w = w + cm(dr, di, sh)
        dr, di = csq(dr, di)
        step *= 2
    h0 = jnp.where(k_idx >= 1, pltpu.roll(w, 1, 0), 0.0)
    h1 = cm(ar, ai, h0) + s0
    hb = jnp.concatenate([h0, h1], axis=0).astype(bf16)
    y = y + lax.dot_general(hb, qt, (((1,), (1,)), ((), ())), preferred_element_type=f32)
    y = jax.nn.gelu(y + e * drow)
    for m in range(H // 2):
        y0 = y[:J, m * LANES:(m + 1) * LANES]
        y1 = y[J:, m * LANES:(m + 1) * LANES]
        o_ref[c0 + 2 * m] = jnp.where(lo, y0, _swap_halves(y1))
        o_ref[c0 + 2 * m + 1] = jnp.where(lo, _swap_halves(y0), y1)


def _s5_mix(u3, toep, p, qt, dec, drow, rows_per_seq):
    J = u3.shape[0]
    G, w, _ = toep.shape
    gps = S5_GROUPS_PER_STEP
    hc = gps * S5_GROUP_CH
    n2 = p.shape[2]
    per_step = lambda a: pl.BlockSpec((gps,) + a.shape[1:], lambda g: (g, 0, 0))
    return pl.pallas_call(
        functools.partial(_s5_mix_kernel, rows_per_seq=rows_per_seq),
        out_shape=jax.ShapeDtypeStruct(u3.shape, f32),
        grid=(G // gps,),
        in_specs=[pl.BlockSpec(memory_space=pl.ANY),
                  per_step(toep), per_step(p), per_step(qt), per_step(dec), per_step(drow)],
        out_specs=pl.BlockSpec(memory_space=pl.ANY),
        scratch_shapes=[pltpu.VMEM((2, hc, J, LANES), f32), pltpu.VMEM((2, hc, J, LANES), f32),
                        pltpu.SemaphoreType.DMA((2,)), pltpu.SemaphoreType.DMA((2,))],
        compiler_params=_cparams("arbitrary"),
        name="s5_mix",
    )(u3, toep, p, qt, dec, drow)


def _split_bf16_rows(v):
    hi = v.astype(bf16)
    lo = (v - hi.astype(f32)).astype(bf16)
    return jnp.concatenate([hi, lo], axis=0)


def _ssd_kernel(xc_ref, dt_ref, zd_ref, alog_ref, dexp_ref, nw_ref, e2_ref, sel_ref, o_ref, state_ref):
    ts = xc_ref.shape[0]
    d = o_ref.shape[1]
    T, N, P = SSD_CHUNK, SSD_STATE, SSD_HEAD_DIM
    gw = d // SSD_GROUPS
    hpg = gw // P
    n_chunks = ts // T

    @pl.when(pl.program_id(1) == 0)
    def _():
        state_ref[...] = jnp.zeros_like(state_ref)

    dtt = dt_ref[...]
    neg_a = -jnp.exp(alog_ref[...])
    li = lax.broadcasted_iota(jnp.int32, (T, T), 0)
    si = lax.broadcasted_iota(jnp.int32, (T, T), 1)
    causal = li >= si
    triu = (li <= si).astype(f32)
    acs_c, w_c = [], []
    for c in range(n_chunks):
        dtc = dtt[:, c * T:(c + 1) * T]
        acs = jnp.dot(dtc * neg_a, triu, precision=lax.Precision.HIGHEST, preferred_element_type=f32)
        acs_c.append(acs)
        w_c.append(jnp.exp(acs[:, T - 1:T] - acs) * dtc)
    acs_t = jnp.concatenate(acs_c, axis=1)
    we_parts = _split_bf16_rows(jnp.concatenate(w_c + [jnp.exp(acs_t)], axis=1))
    ex = lax.dot_general(we_parts, e2_ref[...], (((0,), (0,)), ((), ())), preferred_element_type=f32)
    w_x, e_x = ex[:ts], ex[ts:]
    a_parts = _split_bf16_rows(acs_t)
    n_h = acs_t.shape[0]
    acs_r = a_parts[:n_h].astype(f32) + a_parts[n_h:].astype(f32)
    colb = lax.dot_general(a_parts, sel_ref[...], (((0,), (0,)), ((), ())),
                           preferred_element_type=f32)
    head_of_lane = lax.broadcasted_iota(jnp.int32, (T, gw), 1) // P
    for c in range(n_chunks):
        sl = slice(c * T, (c + 1) * T)
        a_cst = acs_r[:, sl]
        ys = []
        for q in range(SSD_GROUPS):
            cs = slice(q * gw, (q + 1) * gw)
            bq = xc_ref[sl, d + q * N:d + (q + 1) * N]
            cq = xc_ref[sl, d + SSD_GROUPS * N + q * N:d + SSD_GROUPS * N + (q + 1) * N]
            xq = xc_ref[sl, cs]
            g = lax.dot_general(cq, bq, (((1,), (1,)), ((), ())), preferred_element_type=f32)
            scores, xbd = [], []
            for hh in range(hpg):
                h = q * hpg + hh
                diff = colb[sl, h * T:(h + 1) * T] - a_cst[h:h + 1, :]
                lm = jnp.exp(jnp.where(causal, diff, -1e30))
                scores.append((g * lm * dtt[h:h + 1, sl]).astype(bf16))
                xbd.append(jnp.where(head_of_lane == hh, xq, jnp.zeros_like(xq)))
            y_diag = jnp.dot(jnp.concatenate(scores, axis=1), jnp.concatenate(xbd, axis=0),
                             preferred_element_type=f32)
            st = state_ref[:, cs]
            y_off = jnp.dot(cq, st.astype(bf16), preferred_element_type=f32) * e_x[sl, cs]
            xw = xq * w_x[sl, cs].astype(bf16)
            s_new = lax.dot_general(bq, xw, (((0,), (0,)), ((), ())), preferred_element_type=f32)
            state_ref[:, cs] = st * e_x[(c + 1) * T - 1:(c + 1) * T, cs] + s_new
            ys.append(y_diag + y_off)
        y = jnp.concatenate(ys, axis=1) + dexp_ref[...] * xc_ref[sl, :d].astype(f32)
        z = zd_ref[sl, :].astype(f32)
        gt = y * (z * jax.nn.sigmoid(z))
        gn = []
        for q in range(SSD_GROUPS):
            gq = gt[:, q * gw:(q + 1) * gw]
            gn.append(gq * lax.rsqrt(jnp.sum(gq * gq, axis=-1, keepdims=True) * (1.0 / gw) + EPS))
        o_ref[sl, :] = (jnp.concatenate(gn, axis=1) * nw_ref[...]).astype(o_ref.dtype)


def _ssd(xc, dt, zd, alog, dexp, nw, e2, sel, batch, ts):
    m, cdim = xc.shape
    d = zd.shape[1]
    nb = m // batch // ts
    tok = lambda b, i: (b * nb + i, 0)
    return pl.pallas_call(
        _ssd_kernel,
        out_shape=jax.ShapeDtypeStruct((m, d), bf16),
        grid=(batch, nb),
        in_specs=[pl.BlockSpec((ts, cdim), tok), pl.BlockSpec((dt.shape[0], ts), lambda b, i: (0, b * nb + i)),
                  pl.BlockSpec((ts, d), tok), _const_spec(alog.shape), _const_spec(dexp.shape), _const_spec(nw.shape), _const_spec(e2.shape),
                  _const_spec(sel.shape)],
        out_specs=pl.BlockSpec((ts, d), tok),
        scratch_shapes=[pltpu.VMEM((SSD_STATE, d), f32)],
        compiler_params=_cparams("arbitrary", "arbitrary"),
        name="ssd",
    )(xc, dt, zd, alog, dexp, nw, e2, sel)


def _out_kernel(x_ref, y3_ref, zs_ref, yd_ref, p_ref, wglu_ref, bglu_ref, wo_ref, pnw_ref, wg_ref, wp_ref, fnw_ref,
                o_ref):
    d = x_ref.shape[1]
    y = jnp.concatenate([y3_ref[jj].T for jj in range(y3_ref.shape[0])], axis=0)
    glu = y * jax.nn.sigmoid(jnp.dot(y.astype(bf16), wglu_ref[...], preferred_element_type=f32) + bglu_ref[...])
    z = zs_ref[...].astype(f32)
    ys5 = glu * (z * jax.nn.sigmoid(z))
    h = x_ref[...] + jnp.dot(ys5.astype(bf16), wo_ref[:d, :], preferred_element_type=f32)
    h = h + jnp.dot(yd_ref[...], wo_ref[d:, :], preferred_element_type=f32)
    gate = jax.nn.sigmoid(jnp.dot(_rms(h, pnw_ref[...]).astype(bf16), wg_ref[...], preferred_element_type=f32))
    h = h + jnp.dot(p_ref[...].astype(bf16), wp_ref[...], preferred_element_type=f32) * gate
    o_ref[...] = _rms(h, fnw_ref[...])


def _out_proj(x2, y3, zs, yd, p2, wglu, bglu, wo, pnw, wg, wp, fnw, tb):
    m, d = x2.shape
    tok = lambda i: (i, 0)
    return pl.pallas_call(
        _out_kernel,
        out_shape=jax.ShapeDtypeStruct((m, d), f32),
        grid=(m // tb,),
        in_specs=[pl.BlockSpec((tb, d), tok),
                  pl.BlockSpec((tb // LANES, d, LANES), lambda i: (i, 0, 0)),
                  pl.BlockSpec((tb, d), tok), pl.BlockSpec((tb, d), tok), pl.BlockSpec((tb, p2.shape[1]), tok),
                  _const_spec(wglu.shape), _const_spec(bglu.shape), _const_spec(wo.shape), _const_spec(pnw.shape),
                  _const_spec(wg.shape), _const_spec(wp.shape), _const_spec(fnw.shape)],
        out_specs=pl.BlockSpec((tb, d), tok),
        compiler_params=_cparams("parallel"),
        name="out_proj",
    )(x2, y3, zs, yd, p2, wglu, bglu, wo, pnw, wg, wp, fnw)


def _block(m, want):
    b = want
    while m % b:
        b //= 2
    return b


def kernel(x, p, norm_w, w_in, s5_A_re, s5_A_im, s5_log_dt, s5_B_re, s5_B_im, s5_C_re, s5_C_im, s5_D, s5_w_glu, s5_b_glu, conv_w, conv_b, dt_bias, A_log, ssd_D, ssd_norm_w, w_out, ple_norm_w, w_ple_gate, w_ple_proj, final_norm_w):
    batch, seq, d = x.shape
    assert norm_w.shape[0] == 1, "single-layer problem: the final norm is fused into the layer's last kernel"
    m = batch * seq
    n_heads = dt_bias.shape[1]
    assert seq % (2 * LANES) == 0 and d % LANES == 0
    rows_per_seq = seq // LANES
    assert rows_per_seq & (rows_per_seq - 1) == 0
    x2 = x.reshape(m, d)
    toep, pmat, qt, dec = _s5_params(s5_A_re[0], s5_A_im[0], s5_log_dt[0], s5_B_re[0], s5_B_im[0],
                                     s5_C_re[0], s5_C_im[0])
    wi = w_in[0].astype(bf16)
    wu = jnp.pad(jnp.concatenate([wi[:, :d], wi[:, 5 * d:]], axis=1), ((0, 0), (0, LANES - n_heads)))
    wr = wi[:, d:5 * d]
    cw8 = jnp.broadcast_to(conv_w[0][:, None, :], (CONV_WIDTH, 8, conv_w.shape[2]))
    cb8 = jnp.broadcast_to(conv_b[0][None, :], (8, conv_b.shape[1]))
    dtb = jnp.broadcast_to(dt_bias[0][:, None], (n_heads, 2 * LANES))
    u3, zs, zd, xc, dt = _in_proj(x2, norm_w[0][None, :], wu, wr, cw8, cb8, dtb, _block(seq, 512), seq)
    drow = jnp.repeat(s5_D[0].reshape(-1, S5_GROUP_CH), S5_CHUNK, axis=1)[:, None, :]
    y3 = _s5_mix(u3, toep, pmat, qt, dec, drow, rows_per_seq)
    head_of = jnp.arange(d) // SSD_HEAD_DIM
    e2 = (jnp.arange(n_heads)[:, None] == head_of[None, :]).astype(bf16)
    e2 = jnp.concatenate([e2, e2], axis=0)
    tile_of = jnp.arange(n_heads * SSD_CHUNK) // SSD_CHUNK
    sel = (jnp.arange(n_heads)[:, None] == tile_of[None, :]).astype(bf16)
    sel = jnp.concatenate([sel, sel], axis=0)
    alog = jnp.broadcast_to(A_log[0][:, None], (n_heads, SSD_CHUNK))
    yd = _ssd(xc, dt, zd, alog, jnp.repeat(ssd_D[0], SSD_HEAD_DIM)[None, :], ssd_norm_w[0][None, :], e2,
              sel, batch, _block(seq, 512))
    out = _out_proj(x2, y3, zs, yd, p[0].reshape(m, -1), s5_w_glu[0].astype(bf16), s5_b_glu[0][None, :],
                    w_out[0].astype(bf16), ple_norm_w[0][None, :], w_ple_gate[0].astype(bf16),
                    w_ple_proj[0].astype(bf16), final_norm_w[None, :], _block(m, 512))
    return out.reshape(batch, seq, d)
```
